```python
import math
import jax
import jax.numpy as jnp
from jax import lax
import numpy as np

D_MODEL = 1024
BATCH = 32
SEQ = 2048
DEPTH = 2

CTX_LEN = 256
GRID_W = 64

GLA_HEADS = 6
GLA_DK = 32
GLA_DV = 64
GLA_GATE_RANK = 16
GLA_TAU = 16.0
GLA_CHUNK = 64
NAT_HEADS = 4
NAT_DH = 64
NAT_WIN_R = 8
NAT_WIN_C = 16
RW_HEADS = 6
RW_DH = 64
RW_DECAY_RANK = 64
RW_A_RANK = 64
RW_GATE_RANK = 128
RW_GN_EPS = 64e-5
RW_DECAY_SCALE = math.exp(-0.5)

ROPE_BASE = 10000.0
LN_EPS = 1e-5

GLA_QK = GLA_HEADS * GLA_DK
GLA_V = GLA_HEADS * GLA_DV
NAT_W = NAT_HEADS * NAT_DH
RW_W = RW_HEADS * RW_DH
MIX_WIDTH = GLA_V + NAT_W + RW_W
FFN_HIDDEN = ((8 * D_MODEL + 3 * 256 - 1) // (3 * 256)) * 256

GLA_SIZES = (GLA_QK, GLA_QK, GLA_V, GLA_V, 2 * GLA_GATE_RANK)
NAT_SIZES = (NAT_W, NAT_W, NAT_W)
RW_SIZES = (RW_W, RW_W, RW_W, 2 * RW_DECAY_RANK, 2 * RW_A_RANK, RW_GATE_RANK)
GLA_COLS = 2 * GLA_QK + 2 * GLA_V + 2 * GLA_GATE_RANK
NAT_COLS = 3 * NAT_W
RW_COLS = 3 * RW_W + 2 * RW_DECAY_RANK + 2 * RW_A_RANK + RW_GATE_RANK
IN_COLS = GLA_COLS + NAT_COLS + RW_COLS

kernel_name = 'hybrid_gla_natten_rwkv7_dit_block'


def split_cols(t, sizes):
    return jnp.split(t, [int(s) for s in np.cumsum(sizes)[:-1]], axis=-1)


def layer_norm(x, w, b):
    xf = x.astype(jnp.float32)
    mu = jnp.mean(xf, axis=-1, keepdims=True)
    var = jnp.mean(jnp.square(xf - mu), axis=-1, keepdims=True)
    return (xf - mu) * lax.rsqrt(var + LN_EPS) * w.astype(jnp.float32) + b.astype(jnp.float32)


def rms_norm(x, w):
    xf = x.astype(jnp.float32)
    return xf * lax.rsqrt(jnp.mean(jnp.square(xf), axis=-1, keepdims=True) + LN_EPS) * w.astype(jnp.float32)


def group_norm_heads(x, w, b):
    xf = x.astype(jnp.float32)
    mu = jnp.mean(xf, axis=-1, keepdims=True)
    var = jnp.mean(jnp.square(xf - mu), axis=-1, keepdims=True)
    y = (xf - mu) * lax.rsqrt(var + RW_GN_EPS)
    return y * w.reshape(RW_HEADS, RW_DH).astype(jnp.float32) + b.reshape(RW_HEADS, RW_DH).astype(jnp.float32)


def ada_modulation(cvec, w_mod, b_mod):
    m = jax.nn.silu(cvec) @ w_mod + b_mod
    return jnp.split(m, 6, axis=-1)


def axial_rope_tables(n_tok, dim):
    t = jnp.arange(n_tok, dtype=jnp.int32)
    row = (t // GRID_W).astype(jnp.float32)
    col = (t % GRID_W).astype(jnp.float32)
    n_freq = dim // 4
    inv = ROPE_BASE ** (-jnp.arange(n_freq, dtype=jnp.float32) / n_freq)
    ang = jnp.concatenate([row[:, None] * inv, col[:, None] * inv], axis=-1)
    return jnp.cos(ang), jnp.sin(ang)


def apply_rope(x, cos, sin):
    xf = x.astype(jnp.float32)
    x1, x2 = xf[..., 0::2], xf[..., 1::2]
    c, s = cos[:, None, :], sin[:, None, :]
    return jnp.stack([x1 * c - x2 * s, x1 * s + x2 * c], axis=-1).reshape(x.shape)


def gla_chunked(q, k, v, logd, s0, with_out=True):
    B, H, L, DK = q.shape
    DV = v.shape[-1]
    C = GLA_CHUNK
    N = L // C
    q, k, v = (t.astype(jnp.float32).reshape(B, H, N, C, t.shape[-1]) for t in (q, k, v))
    b = jnp.cumsum(logd.astype(jnp.float32).reshape(B, H, N, C, DK), axis=3)
    b_last = b[:, :, :, -1]
    chunk_kv = jnp.einsum('bhnck,bhncv->bhnkv', k * jnp.exp(b_last[:, :, :, None] - b), v)

    def step(S, inp):
        dec, kv = inp
        return dec[..., None] * S + kv, S

    s_final, s_start = lax.scan(step, s0, (jnp.moveaxis(jnp.exp(b_last), 2, 0), jnp.moveaxis(chunk_kv, 2, 0)))
    if not with_out:
        return None, s_final
    q_e = q * jnp.exp(b)
    k_e = k * jnp.exp(-b)
    lower = jnp.tril(jnp.ones((C, C), dtype=bool))
    att = jnp.where(lower, jnp.einsum('bhnik,bhnjk->bhnij', q_e, k_e), 0.0)
    o = (jnp.einsum('bhnij,bhnjv->bhniv', att, v)
         + jnp.einsum('bhnik,bhnkv->bhniv', q_e, jnp.moveaxis(s_start, 0, 2)))
    return o.reshape(B, H, L, DV), s_final


def gla_two_way(q, k, v, logd2, s0f, s0b, with_out=True):
    of, sf = gla_chunked(q, k, v, logd2[0], s0f, with_out)
    rev = lambda t: jnp.flip(t, axis=2)
    ob, sb = gla_chunked(rev(q), rev(k), rev(v), rev(logd2[1]), s0b, with_out)
    o = of + rev(ob) if with_out else None
    return o, sf, sb


def gla_group(parts_l, parts_c, gate_up, gate_b, norm_w, rope, ctx_out):
    def prep(parts, rope_tab):
        q, k, v, g, dn = parts
        B, L, _ = q.shape
        q = q.reshape(B, L, GLA_HEADS, GLA_DK)
        k = k.reshape(B, L, GLA_HEADS, GLA_DK)
        if rope_tab is not None:
            q = apply_rope(q, *rope_tab)
            k = apply_rope(k, *rope_tab)
        z = jnp.einsum('blgr,grk->gblk', dn.reshape(B, L, 2, GLA_GATE_RANK), gate_up) + gate_b[:, None, None, :]
        logd = jax.nn.log_sigmoid(z.astype(jnp.float32)) / GLA_TAU
        logd = logd.reshape(2, B, L, GLA_HEADS, GLA_DK).transpose(0, 1, 3, 2, 4)
        bhld = lambda t: jnp.swapaxes(t, 1, 2)
        return (bhld(q) * GLA_DK ** -0.5, bhld(k), bhld(v.reshape(B, L, GLA_HEADS, GLA_DV)), logd, g)

    def readout(o, g):
        o = jnp.swapaxes(o, 1, 2)
        B, L = o.shape[:2]
        gate = jax.nn.silu(g.reshape(B, L, GLA_HEADS, GLA_DV).astype(jnp.float32))
        return (rms_norm(o, norm_w) * gate).reshape(B, L, GLA_V)

    qc, kc, vc, dc, gc = prep(parts_c, None)
    s0 = jnp.zeros((qc.shape[0], GLA_HEADS, GLA_DK, GLA_DV), jnp.float32)
    oc, sf, sb = gla_two_way(qc, kc, vc, dc, s0, s0, ctx_out)
    ql, kl, vl, dl, gl = prep(parts_l, rope)
    ol, _, _ = gla_two_way(ql, kl, vl, dl, sf, sb)
    return readout(ol, gl), (readout(oc, gc) if ctx_out else None)


def nat_group(parts_l, parts_c, rpb, ctx_out):
    ql, kl, vl = (t.reshape(t.shape[0], t.shape[1], NAT_HEADS, NAT_DH) for t in parts_l)
    qc, kc, vc = (t.reshape(t.shape[0], t.shape[1], NAT_HEADS, NAT_DH) for t in parts_c)
    B, L = ql.shape[:2]
    R = L // GRID_W
    KR = min(NAT_WIN_R, R)
    KC = NAT_WIN_C
    scale = NAT_DH ** -0.5
    r_idx = jnp.arange(R)
    rows = jnp.clip(r_idx - KR // 2, 0, R - KR)[:, None] + jnp.arange(KR)[None, :]
    cq = jnp.arange(GRID_W)
    col0 = jnp.clip(cq - KC // 2, 0, GRID_W - KC)
    in_win = (cq[None, :] >= col0[:, None]) & (cq[None, :] < col0[:, None] + KC)
    dr = rows - r_idx[:, None] + NAT_WIN_R - 1
    dc = jnp.clip(cq[None, :] - cq[:, None], -(KC - 1), KC - 1) + KC - 1
    bias = rpb[:, dr][:, :, :, dc].transpose(0, 1, 3, 2, 4)

    qg = ql.reshape(B, R, GRID_W, NAT_HEADS, NAT_DH)
    kg = kl.reshape(B, R, GRID_W, NAT_HEADS, NAT_DH)[:, rows]
    vg = vl.reshape(B, R, GRID_W, NAT_HEADS, NAT_DH)[:, rows]
    s_loc = jnp.einsum('brqhd,brkwhd->bhrqkw', qg, kg).astype(jnp.float32) * scale + bias
    s_loc = jnp.where(in_win[:, None, :], s_loc, -jnp.inf)
    s_ctx = jnp.einsum('brqhd,bchd->bhrqc', qg, kc).astype(jnp.float32) * scale
    n_loc = KR * GRID_W
    s = jnp.concatenate([s_loc.reshape(B, NAT_HEADS, R, GRID_W, n_loc), s_ctx], axis=-1)
    p = jax.nn.softmax(s, axis=-1)
    p_loc = p[..., :n_loc].reshape(B, NAT_HEADS, R, GRID_W, KR, GRID_W)
    o = (jnp.einsum('bhrqkw,brkwhd->brqhd', p_loc, vg.astype(jnp.float32))
         + jnp.einsum('bhrqc,bchd->brqhd', p[..., n_loc:], vc.astype(jnp.float32)))
    out_l = o.reshape(B, L, NAT_W)
    if not ctx_out:
        return out_l, None
    sc = jnp.einsum('bihd,bjhd->bhij', qc, kc).astype(jnp.float32) * scale
    oc = jnp.einsum('bhij,bjhd->bihd', jax.nn.softmax(sc, axis=-1), vc.astype(jnp.float32))
    return out_l, oc.reshape(B, qc.shape[1], NAT_W)


def centred_shift(y, mu):
    pad = jnp.pad(y, ((0, 0), (1, 1), (0, 0)))
    return y + (0.5 * (pad[:, :-2] + pad[:, 2:]) - y) * mu


def rwkv7_scan(r, w, k, v, kk, a, s0):
    def step(S, inp):
        r_t, w_t, k_t, v_t, kk_t, a_t = inp
        sa = jnp.einsum('bhvk,bhk->bhv', S, -kk_t)
        S = (S * w_t[:, :, None, :] + sa[..., None] * (kk_t * a_t)[:, :, None, :]
             + v_t[..., None] * k_t[:, :, None, :])
        return S, jnp.einsum('bhvk,bhk->bhv', S, r_t)

    s_final, o = lax.scan(step, s0, tuple(jnp.moveaxis(t, 1, 0) for t in (r, w, k, v, kk, a)))
    return jnp.moveaxis(o, 0, 1), s_final


def rwkv_group(P_l, P_c, mu, w0, wd2, a0, wa2, wg2, k_k, k_a, r_k, gn_w, gn_b, ctx_out):
    def heads(t):
        return t.reshape(t.shape[:-1] + (RW_HEADS, RW_DH)).astype(jnp.float32)

    def prep(P):
        y = centred_shift(P, mu)
        r, k, v, dd, ad, gd = split_cols(y, RW_SIZES)
        B, L, _ = r.shape
        d = w0[:, None, None] + jnp.einsum('blgr,grc->gblc', jnp.tanh(dd.reshape(B, L, 2, RW_DECAY_RANK)), wd2)
        w = jnp.exp(-RW_DECAY_SCALE * jax.nn.sigmoid(d.astype(jnp.float32)))
        a = jax.nn.sigmoid((a0[:, None, None] + jnp.einsum('blgr,grc->gblc', ad.reshape(B, L, 2, RW_A_RANK), wa2)).astype(jnp.float32))
        g = jax.nn.sigmoid(gd) @ wg2
        kk = heads(k * k_k)
        kk = kk * lax.rsqrt(jnp.sum(kk * kk, axis=-1, keepdims=True) + 1e-12)
        kmod = k[None].astype(jnp.float32) * (1.0 + (a - 1.0) * k_a)
        return heads(r), heads(w), heads(kmod), heads(v), kk, heads(a), g

    def two_way(st, s0f, s0b):
        r, w, k, v, kk, a, _ = st
        of, sf = rwkv7_scan(r, w[0], k[0], v, kk, a[0], s0f)
        rev = lambda t: jnp.flip(t, axis=1)
        ob, sb = rwkv7_scan(rev(r), rev(w[1]), rev(k[1]), rev(v), rev(kk), rev(a[1]), s0b)
        return of + rev(ob), sf, sb

    def readout(o, st):
        r, _, k, v, _, _, g = st
        B, L = o.shape[:2]
        rk = r_k.reshape(RW_HEADS, RW_DH).astype(jnp.float32)
        bonus = jnp.sum(r[None] * k * rk, axis=-1, keepdims=True).sum(0) * v
        return (group_norm_heads(o, gn_w, gn_b) + bonus).reshape(B, L, RW_W) * g

    st_c = prep(P_c)
    s0 = jnp.zeros((P_c.shape[0], RW_HEADS, RW_DH, RW_DH), jnp.float32)
    oc, sf, sb = two_way(st_c, s0, s0)
    st_l = prep(P_l)
    ol, _, _ = two_way(st_l, sf, sb)
    return readout(ol, st_l), (readout(oc, st_c) if ctx_out else None)


def swiglu(h, w13, w2):
    gte, up = jnp.split(h @ w13, 2, axis=-1)
    return (jax.nn.silu(gte) * up) @ w2


def token_mixers(P, Pc, p, rope, ctx_out):
    gl_, nl_, rl_ = split_cols(P, (GLA_COLS, NAT_COLS, RW_COLS))
    gc_, nc_, rc_ = split_cols(Pc, (GLA_COLS, NAT_COLS, RW_COLS))
    a_l, a_c = gla_group(split_cols(gl_, GLA_SIZES), split_cols(gc_, GLA_SIZES),
                         p['gla_gate_up'], p['gla_gate_b'], p['gla_norm_w'], rope, ctx_out)
    b_l, b_c = nat_group(split_cols(nl_, NAT_SIZES), split_cols(nc_, NAT_SIZES), p['nat_rpb'], ctx_out)
    c_l, c_c = rwkv_group(rl_, rc_, p['rw_mu'], p['rw_w0'], p['rw_wd2'], p['rw_a0'], p['rw_wa2'], p['rw_wg2'],
                          p['rw_k_k'], p['rw_k_a'], p['rw_r_k'], p['rw_gn_w'], p['rw_gn_b'], ctx_out)
    y_l = jnp.concatenate([a_l, b_l, c_l], axis=-1)
    y_c = jnp.concatenate([a_c, b_c, c_c], axis=-1) if ctx_out else None
    return y_l, y_c


def hybrid_layer(xl, xc, c, c_ctx, p, rope, last):
    dt = xl.dtype
    alpha = (2.0 * DEPTH) ** 0.25
    ctx_out = not last
    ml = [m[:, None, :] for m in ada_modulation(c, p['w_mod'], p['b_mod'])]
    mc = ada_modulation(c_ctx, p['w_mod'], p['b_mod'])
    P = (xl * (1.0 + ml[1]) + ml[0]) @ p['w_in']
    Pc = (xc * (1.0 + mc[1]) + mc[0]) @ p['w_in']
    y_l, y_c = token_mixers(P, Pc, p, rope, ctx_out)
    xl = layer_norm(alpha * xl + ml[2] * (y_l.astype(dt) @ p['w_out']), p['ln1_w'], p['ln1_b']).astype(dt)
    f_l = swiglu(xl * (1.0 + ml[4]) + ml[3], p['ffn_w13'], p['ffn_w2'])
    xl = layer_norm(alpha * xl + ml[5] * f_l, p['ln2_w'], p['ln2_b']).astype(dt)
    if ctx_out:
        xc = layer_norm(alpha * xc + mc[2] * (y_c.astype(dt) @ p['w_out']), p['ln1_w'], p['ln1_b']).astype(dt)
        f_c = swiglu(xc * (1.0 + mc[4]) + mc[3], p['ffn_w13'], p['ffn_w2'])
        xc = layer_norm(alpha * xc + mc[5] * f_c, p['ln2_w'], p['ln2_b']).astype(dt)
    return xl, xc


def setup_inputs(seed: int = 0) -> dict:
    key = jax.random.key(seed)
    ks = jax.random.split(key, 40)
    f32 = jnp.float32
    D = D_MODEL
    beta = (8.0 * DEPTH) ** -0.25

    def nrm(k, shape, s):
        return jax.random.normal(k, shape, f32) * s

    return {
        'x': nrm(ks[0], (BATCH, SEQ, D), 1.0),
        'c': nrm(ks[1], (BATCH, D), 1.0),
        'ctx': nrm(ks[2], (BATCH, CTX_LEN, D), 1.0),
        'c_ctx': nrm(ks[3], (D,), 1.0),
        'w_mod': nrm(ks[4], (DEPTH, D, 6 * D), 0.5 * D ** -0.5),
        'b_mod': nrm(ks[5], (DEPTH, 6 * D), 0.02),
        'w_in': nrm(ks[6], (DEPTH, D, IN_COLS), D ** -0.5),
        'gla_gate_up': nrm(ks[7], (DEPTH, 2, GLA_GATE_RANK, GLA_QK), GLA_GATE_RANK ** -0.5),
        'gla_gate_b': nrm(ks[8], (DEPTH, 2, GLA_QK), 0.5),
        'gla_norm_w': 1.0 + nrm(ks[9], (DEPTH, GLA_DV), 0.02),
        'nat_rpb': nrm(ks[10], (DEPTH, NAT_HEADS, 2 * NAT_WIN_R - 1, 2 * NAT_WIN_C - 1), 0.02),
        'rw_mu': jax.random.uniform(ks[11], (DEPTH, RW_COLS), f32),
        'rw_w0': jax.random.uniform(ks[12], (DEPTH, 2, RW_W), f32, -3.0, 1.0),
        'rw_wd2': nrm(ks[13], (DEPTH, 2, RW_DECAY_RANK, RW_W), 0.5 * RW_DECAY_RANK ** -0.5),
        'rw_a0': nrm(ks[14], (DEPTH, 2, RW_W), 0.1),
        'rw_wa2': nrm(ks[15], (DEPTH, 2, RW_A_RANK, RW_W), 0.5 * RW_A_RANK ** -0.5),
        'rw_wg2': nrm(ks[16], (DEPTH, RW_GATE_RANK, RW_W), RW_GATE_RANK ** -0.5),
        'rw_k_k': 0.85 + nrm(ks[17], (DEPTH, RW_W), 0.02),
        'rw_k_a': 1.0 + nrm(ks[18], (DEPTH, RW_W), 0.02),
        'rw_r_k': nrm(ks[19], (DEPTH, RW_W), 0.1),
        'rw_gn_w': 1.0 + nrm(ks[20], (DEPTH, RW_W), 0.02),
        'rw_gn_b': nrm(ks[21], (DEPTH, RW_W), 0.02),
        'w_out': nrm(ks[22], (DEPTH, MIX_WIDTH, D), beta * MIX_WIDTH ** -0.5),
        'ln1_w': 1.0 + nrm(ks[23], (DEPTH, D), 0.02),
        'ln1_b': nrm(ks[24], (DEPTH, D), 0.02),
        'ffn_w13': nrm(ks[25], (DEPTH, D, 2 * FFN_HIDDEN), D ** -0.5),
        'ffn_w2': nrm(ks[26], (DEPTH, FFN_HIDDEN, D), beta * FFN_HIDDEN ** -0.5),
        'ln2_w': 1.0 + nrm(ks[27], (DEPTH, D), 0.02),
        'ln2_b': nrm(ks[28], (DEPTH, D), 0.02),
    }


def reference(x, c, ctx, c_ctx, w_mod, b_mod, w_in, gla_gate_up, gla_gate_b, gla_norm_w, nat_rpb,
              rw_mu, rw_w0, rw_wd2, rw_a0, rw_wa2, rw_wg2, rw_k_k, rw_k_a, rw_r_k, rw_gn_w, rw_gn_b,
              w_out, ln1_w, ln1_b, ffn_w13, ffn_w2, ln2_w, ln2_b):
    rope = axial_rope_tables(x.shape[1], GLA_DK)
    xl, xc = x, ctx
    for i in range(DEPTH):
        p = dict(w_mod=w_mod[i], b_mod=b_mod[i], w_in=w_in[i], gla_gate_up=gla_gate_up[i], gla_gate_b=gla_gate_b[i],
                 gla_norm_w=gla_norm_w[i], nat_rpb=nat_rpb[i], rw_mu=rw_mu[i], rw_w0=rw_w0[i], rw_wd2=rw_wd2[i],
                 rw_a0=rw_a0[i], rw_wa2=rw_wa2[i], rw_wg2=rw_wg2[i], rw_k_k=rw_k_k[i], rw_k_a=rw_k_a[i],
                 rw_r_k=rw_r_k[i], rw_gn_w=rw_gn_w[i], rw_gn_b=rw_gn_b[i], w_out=w_out[i], ln1_w=ln1_w[i],
                 ln1_b=ln1_b[i], ffn_w13=ffn_w13[i], ffn_w2=ffn_w2[i], ln2_w=ln2_w[i], ln2_b=ln2_b[i])
        xl, xc = hybrid_layer(xl, xc, c, c_ctx, p, rope, i == DEPTH - 1)
    return xl
```

```python
import functools
import math

import jax
import jax.numpy as jnp
import numpy as np
from jax import lax
from jax.experimental import pallas as pl
from jax.experimental.pallas import tpu as pltpu

F32 = jnp.float32
BF16 = jnp.bfloat16

GRID_W = 64
GLA_HEADS, GLA_DK, GLA_DV, GLA_GATE_RANK, GLA_TAU = 6, 32, 64, 16, 16.0
NAT_HEADS, NAT_DH, NAT_WIN_R, NAT_WIN_C = 4, 64, 8, 16
RW_HEADS, RW_DH, RW_DECAY_RANK, RW_A_RANK, RW_GATE_RANK = 6, 64, 64, 64, 128
RW_GN_EPS = 64e-5
RW_DECAY_SCALE = math.exp(-0.5)
ROPE_BASE = 10000.0
LN_EPS = 1e-5

GLA_QK = GLA_HEADS * GLA_DK
GLA_V = GLA_HEADS * GLA_DV
NAT_W = NAT_HEADS * NAT_DH
RW_W = RW_HEADS * RW_DH

LANES = 128
CHUNK = 64
GLA_QK_PAD = 256
GLA_COLS_P = 2 * GLA_QK_PAD + 2 * GLA_V + LANES
NAT_COLS_P = 3 * NAT_W
RW_COLS_P = 3 * RW_W + 3 * LANES
VMEM_LIMIT = 56 * 1024 * 1024

NN = (((1,), (0,)), ((), ()))
NT = (((1,), (1,)), ((), ()))
TN = (((0,), (0,)), ((), ()))


def _dot(a, b, dims=NN):
    return lax.dot_general(a, b, dims, preferred_element_type=F32)


def _bf(x):
    return x.astype(BF16)


def _split(x):
    hi = x.astype(BF16)
    lo = (x - hi.astype(F32)).astype(BF16)
    return hi, lo


def _dot3(a, b, dims=NN):
    ah, al = _split(a)
    bh, bl = _split(b)
    return _dot(ah, bh, dims) + (_dot(ah, bl, dims) + _dot(al, bh, dims))


def _dot_exact_lhs(a_bf, b, dims=NN):
    bh, bl = _split(b)
    return _dot(a_bf, bh, dims) + _dot(a_bf, bl, dims)


def _dot_exact_rhs(a, b_bf, dims=NN):
    ah, al = _split(a)
    return _dot(ah, b_bf, dims) + _dot(al, b_bf, dims)


def _sigmoid(x):
    return 1.0 / (1.0 + jnp.exp(-x))


def _silu(x):
    return x * _sigmoid(x)


def _iota(shape, axis):
    return lax.broadcasted_iota(jnp.int32, shape, axis)


def _div_pow2(x, n):
    return lax.shift_right_logical(x, int(math.log2(n)))


def _mod_pow2(x, n):
    return lax.bitwise_and(x, n - 1)


def _layer_norm(h, w, b):
    mu = jnp.mean(h, axis=-1, keepdims=True)
    xc = h - mu
    var = jnp.mean(xc * xc, axis=-1, keepdims=True)
    return xc * lax.rsqrt(var + LN_EPS) * w + b


def _params(*sem):
    return pltpu.CompilerParams(dimension_semantics=sem, vmem_limit_bytes=VMEM_LIMIT)


def _const_spec(shape):
    nd = len(shape)
    return pl.BlockSpec(shape, lambda *_: (0,) * nd)


def _mod_kernel(c_ref, w_ref, b_ref, o_ref):
    o_ref[...] = _dot3(_silu(c_ref[...]), w_ref[...]) + b_ref[...]


def _modulation(cvec, w_mod, b_mod):
    rows, d = cvec.shape
    n = w_mod.shape[1]
    tn = 1024
    return pl.pallas_call(
        _mod_kernel,
        grid=(n // tn,),
        in_specs=[pl.BlockSpec((rows, d), lambda j: (0, 0)),
                  pl.BlockSpec((d, tn), lambda j: (0, j)),
                  pl.BlockSpec((1, tn), lambda j: (0, j))],
        out_specs=pl.BlockSpec((rows, tn), lambda j: (0, j)),
        out_shape=jax.ShapeDtypeStruct((rows, n), F32),
        compiler_params=_params("arbitrary"),
        name="modulation",
    )(cvec, w_mod, b_mod.reshape(1, n))


def _inproj_kernel(x_ref, mod_ref, w_ref, og_ref, on_ref, or_ref):
    x = x_ref[0]
    sh = mod_ref[0, 0, 0:1, :]
    sc = mod_ref[0, 0, 1:2, :]
    xm = _bf(x * (1.0 + sc) + sh)
    c0, c1 = GLA_COLS_P, GLA_COLS_P + NAT_COLS_P
    og_ref[0] = _dot(xm, w_ref[:, 0:c0])
    on_ref[0] = _dot(xm, w_ref[:, c0:c1])
    or_ref[0] = _dot(xm, w_ref[:, c1:])


def _inproj(xz, modtab, w_in_p, tm, nct):
    b, t, d = xz.shape
    ncols = w_in_p.shape[1]
    seg = lambda bi, j: (bi, jnp.where(j >= nct, 1, 0), 0, 0)
    row = lambda bi, j: (bi, j, 0)
    return pl.pallas_call(
        _inproj_kernel,
        grid=(b, t // tm),
        in_specs=[pl.BlockSpec((1, tm, d), row),
                  pl.BlockSpec((1, 1, 6, d), seg),
                  _const_spec((d, ncols))],
        out_specs=[pl.BlockSpec((1, tm, GLA_COLS_P), row),
                   pl.BlockSpec((1, tm, NAT_COLS_P), row),
                   pl.BlockSpec((1, tm, RW_COLS_P), row)],
        out_shape=[jax.ShapeDtypeStruct((b, t, GLA_COLS_P), F32),
                   jax.ShapeDtypeStruct((b, t, NAT_COLS_P), F32),
                   jax.ShapeDtypeStruct((b, t, RW_COLS_P), F32)],
        compiler_params=_params("arbitrary", "arbitrary"),
        name="inproj",
    )(xz, modtab, w_in_p)


def _tri_ones(direction):
    r = _iota((CHUNK, CHUNK), 0)
    c = _iota((CHUNK, CHUNK), 1)
    keep = (c <= r) if direction == 0 else (c >= r)
    return jnp.where(keep, 1.0, 0.0).astype(BF16)


def _block_ones(width, block):
    r = _div_pow2(_iota((width, width), 0), block)
    c = _div_pow2(_iota((width, width), 1), block)
    return jnp.where(r == c, 1.0, 0.0).astype(BF16)


def _run_two_way(chunk_fn, c_lo, n):
    half = n // 2

    def first(i, carry):
        chunk_fn(c_lo + i, 0, True, c_lo, n)
        chunk_fn(c_lo + n - 1 - i, 1, True, c_lo, n)
        return carry

    def second(i, carry):
        chunk_fn(c_lo + half + i, 0, False, c_lo, n)
        chunk_fn(c_lo + half - 1 - i, 1, False, c_lo, n)
        return carry

    lax.fori_loop(0, half, first, 0)
    lax.fori_loop(0, half, second, 0)


def _gla_kernel(pg_ref, cos_ref, sin_ref, wg_ref, gb_ref, nw_ref, o_ref, acc_ref, s_ref, *, n_ctx, n_all):
    C = CHUNK
    s_ref[...] = jnp.zeros(s_ref.shape, F32)

    lane_k = _iota((1, GLA_QK_PAD), 1)
    lane_v = _iota((1, GLA_V), 1)
    head_k = [_div_pow2(lane_k, GLA_DK) == h for h in range(GLA_HEADS)]
    head_v = [_div_pow2(lane_v, GLA_DV) == h for h in range(GLA_HEADS)]
    even_lane = _mod_pow2(_iota((1, LANES), 1), 2) == 0
    rows = _iota((C, GLA_HEADS * C), 0)
    cols = _mod_pow2(_iota((C, GLA_HEADS * C), 1), C)
    causal = [cols <= rows, cols >= rows]
    tri = [_tri_ones(0), _tri_ones(1)]
    bd_v = _block_ones(GLA_V, GLA_DV)
    s_rows = _div_pow2(_iota((GLA_V, GLA_QK_PAD), 0), GLA_DV)
    s_cols = _div_pow2(_iota((GLA_V, GLA_QK_PAD), 1), GLA_DK)
    s_diag = s_rows == s_cols
    scale = GLA_DK ** -0.5

    def chunk(c, direction, is_first, c_lo, n):
        del c_lo, n
        r0 = pl.multiple_of(c * C, C)
        rs = pl.ds(r0, C)
        cs = cos_ref[rs, :]
        sn = sin_ref[rs, :]

        def rope(col0):
            parts = []
            for j in range(GLA_QK_PAD // LANES):
                xb = pg_ref[0, rs, col0 + j * LANES:col0 + (j + 1) * LANES]
                sw = jnp.where(even_lane, pltpu.roll(xb, LANES - 1, 1), pltpu.roll(xb, 1, 1))
                parts.append(xb * cs + sw * sn)
            return jnp.concatenate(parts, axis=1)

        q = rope(0)
        k = rope(GLA_QK_PAD)
        v = pg_ref[0, rs, 2 * GLA_QK_PAD:2 * GLA_QK_PAD + GLA_V]
        dn = pg_ref[0, rs, 2 * GLA_QK_PAD + 2 * GLA_V:]
        wcols = slice(direction * GLA_QK_PAD, (direction + 1) * GLA_QK_PAD)
        z = _dot3(dn, wg_ref[:, wcols]) + gb_ref[:, wcols]
        logd = jax.nn.log_sigmoid(z) * (1.0 / GLA_TAU)
        bcum = _dot_exact_lhs(tri[direction], logd)
        b_end = bcum[C - 1:C, :] if direction == 0 else bcum[0:1, :]
        q_e = _bf(q * jnp.exp(bcum) * scale)
        k_e = k * jnp.exp(-bcum)
        k_l = _bf(k * jnp.exp(b_end - bcum))

        k_stack = jnp.concatenate([_bf(jnp.where(head_k[h], k_e, 0.0)) for h in range(GLA_HEADS)], axis=0)
        att = _dot(q_e, k_stack, NT)
        att = _bf(jnp.where(causal[direction], att, 0.0))
        v_bd = jnp.concatenate([_bf(jnp.where(head_v[h], v, 0.0)) for h in range(GLA_HEADS)], axis=0)
        state = s_ref[direction]
        o = _dot(att, v_bd) + _dot(q_e, _bf(state), NT)
        kv = _dot(_bf(v), k_l, TN)
        s_ref[direction] = state * jnp.exp(b_end) + jnp.where(s_diag, kv, 0.0)

        if is_first:
            acc_ref[rs, :] = o
        else:
            tot = acc_ref[rs, :] + o
            g = pg_ref[0, rs, 2 * GLA_QK_PAD + GLA_V:2 * GLA_QK_PAD + 2 * GLA_V]
            ms = _dot_exact_rhs(tot * tot, bd_v) * (1.0 / GLA_DV)
            o_ref[0, rs, :] = tot * lax.rsqrt(ms + LN_EPS) * nw_ref[...] * _silu(g)

    _run_two_way(chunk, 0, n_ctx)
    _run_two_way(chunk, n_ctx, n_all - n_ctx)


def _gla(pg, cos_t, sin_t, wg, gb, nw, lc):
    b, t, _ = pg.shape
    kern = functools.partial(_gla_kernel, n_ctx=lc // CHUNK, n_all=t // CHUNK)
    return pl.pallas_call(
        kern,
        grid=(b,),
        in_specs=[pl.BlockSpec((1, t, GLA_COLS_P), lambda i: (i, 0, 0)),
                  _const_spec((t, LANES)), _const_spec((t, LANES)),
                  _const_spec(wg.shape), _const_spec(gb.shape), _const_spec(nw.shape)],
        out_specs=pl.BlockSpec((1, t, GLA_V), lambda i: (i, 0, 0)),
        out_shape=jax.ShapeDtypeStruct((b, t, GLA_V), F32),
        scratch_shapes=[pltpu.VMEM((t, GLA_V), F32),
                        pltpu.VMEM((2, GLA_V, GLA_QK_PAD), F32)],
        compiler_params=_params("arbitrary"),
        name="gla",
    )(pg, cos_t, sin_t, wg, gb, nw)


def _nat_kernel(pn_ref, bias_ref, o_ref, *, lc, n_rows):
    W = GRID_W
    KR = NAT_WIN_R
    scale = NAT_DH ** -0.5
    half = [_iota((1, LANES), 1) < NAT_DH, _iota((1, LANES), 1) >= NAT_DH]
    n_pairs = NAT_W // LANES

    def softmax_pv(parts_s, parts_v):
        m = parts_s[0].max(axis=-1, keepdims=True)
        for s in parts_s[1:]:
            m = jnp.maximum(m, s.max(axis=-1, keepdims=True))
        ps = [jnp.exp(s - m) for s in parts_s]
        l = ps[0].sum(axis=-1, keepdims=True)
        for p in ps[1:]:
            l = l + p.sum(axis=-1, keepdims=True)
        o = _dot(_bf(ps[0]), parts_v[0])
        for p, vv in zip(ps[1:], parts_v[1:]):
            o = o + _dot(_bf(p), vv)
        return o / l

    for j in range(n_pairs):
        cl = slice(j * LANES, (j + 1) * LANES)
        qc = pn_ref[0, 0:lc, cl] * scale
        kc = _bf(pn_ref[0, 0:lc, NAT_W + j * LANES:NAT_W + (j + 1) * LANES])
        vc = _bf(pn_ref[0, 0:lc, 2 * NAT_W + j * LANES:2 * NAT_W + (j + 1) * LANES])
        outs = []
        for s in range(2):
            qm = _bf(jnp.where(half[s], qc, 0.0))
            outs.append(softmax_pv([_dot(qm, kc, NT)], [vc]))
        o_ref[0, 0:lc, cl] = jnp.where(half[0], outs[0], outs[1])

    def row_body(r, carry):
        start = jnp.clip(r - KR // 2, 0, n_rows - KR)
        delta = r - start
        q0 = pl.multiple_of(lc + r * W, W)
        k0 = pl.multiple_of(lc + start * W, W)
        for j in range(n_pairs):
            cl = slice(j * LANES, (j + 1) * LANES)
            kcl = slice(NAT_W + j * LANES, NAT_W + (j + 1) * LANES)
            vcl = slice(2 * NAT_W + j * LANES, 2 * NAT_W + (j + 1) * LANES)
            q = pn_ref[0, pl.ds(q0, W), cl] * scale
            kb = _bf(pn_ref[0, pl.ds(k0, KR * W), kcl])
            vb = _bf(pn_ref[0, pl.ds(k0, KR * W), vcl])
            kc = _bf(pn_ref[0, 0:lc, kcl])
            vc = _bf(pn_ref[0, 0:lc, vcl])
            outs = []
            for s in range(2):
                h = 2 * j + s
                qm = _bf(jnp.where(half[s], q, 0.0))
                s_loc = _dot(qm, kb, NT) + bias_ref[h, delta]
                s_ctx = _dot(qm, kc, NT)
                outs.append(softmax_pv([s_loc, s_ctx], [vb, vc]))
            o_ref[0, pl.ds(q0, W), cl] = jnp.where(half[0], outs[0], outs[1])
        return carry

    lax.fori_loop(0, n_rows, row_body, 0)


def _nat(pn, bias_tab, lc):
    b, t, _ = pn.shape
    kern = functools.partial(_nat_kernel, lc=lc, n_rows=(t - lc) // GRID_W)
    return pl.pallas_call(
        kern,
        grid=(b,),
        in_specs=[pl.BlockSpec((1, t, NAT_COLS_P), lambda i: (i, 0, 0)),
                  _const_spec(bias_tab.shape)],
        out_specs=pl.BlockSpec((1, t, NAT_W), lambda i: (i, 0, 0)),
        out_shape=jax.ShapeDtypeStruct((b, t, NAT_W), F32),
        compiler_params=_params("arbitrary"),
        name="nat",
    )(pn, bias_tab)


def _rw_kernel(pr_ref, mu_ref, w0_ref, wd_ref, a0_ref, wa_ref, wg_ref, kk_ref, ka_ref, rk_ref,
               gnw_ref, gnb_ref, o_ref, acc_ref, s_ref, *, n_ctx, n_all):
    C = CHUNK
    C2 = 2 * C
    n_pairs = RW_W // LANES
    t_all = n_all * C
    s_ref[...] = jnp.zeros(s_ref.shape, F32)

    lane = _iota((1, LANES), 1)
    half = [lane < RW_DH, lane >= RW_DH]
    row_id = _iota((C, 1), 0)
    tri = [_tri_ones(0), _tri_ones(1)]
    bd = _block_ones(RW_W, RW_DH)
    r_c = _iota((C, C2), 0)
    c_c = _mod_pow2(_iota((C, C2), 1), C)
    incl = [c_c <= r_c, c_c >= r_c]
    r_s = _iota((C2, C2), 0)
    c_s = _iota((C2, C2), 1)
    strict = [c_s < r_s, c_s > r_s]
    n_double = int(math.log2(C))

    def seg_sum(x):
        return _dot_exact_rhs(x, bd)

    def lora(x, w_ref, direction):
        return _dot3(x, w_ref[direction])

    def chunk(c, direction, is_first, c_lo, n):
        r0 = pl.multiple_of(c * C, C)
        rs = pl.ds(r0, C)
        p = pr_ref[0, rs, :]
        prev_row = pr_ref[0, pl.ds(pl.multiple_of(jnp.maximum(r0 - 8, 0), 8), 8), :][7:8, :]
        next_row = pr_ref[0, pl.ds(pl.multiple_of(jnp.minimum(r0 + C, t_all - 8), 8), 8), :][0:1, :]
        prev_row = jnp.where(c == c_lo, 0.0, prev_row)
        next_row = jnp.where(c == c_lo + n - 1, 0.0, next_row)
        up = jnp.where(row_id == 0, prev_row, pltpu.roll(p, 1, 0))
        dn = jnp.where(row_id == C - 1, next_row, pltpu.roll(p, C - 1, 0))
        y = p + (0.5 * (up + dn) - p) * mu_ref[...]

        r = y[:, 0:RW_W]
        k = y[:, RW_W:2 * RW_W]
        v = y[:, 2 * RW_W:3 * RW_W]
        dd = y[:, 3 * RW_W:3 * RW_W + LANES]
        ad = y[:, 3 * RW_W + LANES:3 * RW_W + 2 * LANES]

        d = w0_ref[direction] + lora(jnp.tanh(dd), wd_ref, direction)
        logw = -RW_DECAY_SCALE * _sigmoid(d)
        asig = _sigmoid(a0_ref[direction] + lora(ad, wa_ref, direction))
        kk = k * kk_ref[...]
        kk = kk * lax.rsqrt(seg_sum(kk * kk) + 1e-12)
        kmod = k * (1.0 + (asig - 1.0) * ka_ref[...])

        gcum = _dot_exact_lhs(tri[direction], logw)
        g_end = gcum[C - 1:C, :] if direction == 0 else gcum[0:1, :]
        e_pos = jnp.exp(gcum)
        e_neg = jnp.exp(-gcum)
        r_t = r * e_pos
        a_t = -kk * jnp.exp(gcum - logw)
        b_t = kk * asig * e_neg
        k_t = kmod * e_neg
        decay_end = jnp.exp(g_end)

        outs = []
        for j in range(n_pairs):
            cl = slice(j * LANES, (j + 1) * LANES)

            def stack(x):
                xb = x[:, cl]
                return jnp.concatenate([jnp.where(half[0], xb, 0.0), jnp.where(half[1], xb, 0.0)], axis=0)

            a_st = stack(a_t)
            v_st = _bf(stack(v))
            rhs = _bf(jnp.concatenate([stack(b_t), stack(k_t)], axis=0))
            lhs = _bf(jnp.concatenate([r_t[:, cl], a_st], axis=0))
            sc = _dot(lhs, rhs, NT)
            a_rb = jnp.where(incl[direction], sc[0:C, 0:C2], 0.0)
            a_rk = jnp.where(incl[direction], sc[0:C, C2:], 0.0)
            n_bd = jnp.where(strict[direction], sc[C:, 0:C2], 0.0)
            a_ak = jnp.where(strict[direction], sc[C:, C2:], 0.0)
            y2 = _dot(_bf(a_ak), v_st)
            x = jnp.concatenate([a_st, y2], axis=1)
            npow = n_bd
            x = x + _dot(_bf(npow), _bf(x))
            for _ in range(n_double - 1):
                npow = _dot(_bf(npow), _bf(npow))
                x = x + _dot(_bf(npow), _bf(x))
            a_hat = x[:, 0:LANES]
            u_hat = x[:, LANES:]
            state = s_ref[direction, j]
            m3 = _dot(_bf(jnp.concatenate([r_t[:, cl], a_hat], axis=0)), _bf(state), NT)
            u = m3[C:] + u_hat
            uv = jnp.concatenate([_bf(u), v_st], axis=0)
            outs.append(m3[0:C] + _dot(_bf(jnp.concatenate([a_rb, a_rk], axis=1)), uv))
            s_ref[direction, j] = (state + _dot(uv, rhs, TN)) * decay_end[:, cl]
        o = jnp.concatenate(outs, axis=1)

        if is_first:
            acc_ref[rs, :] = o
        else:
            tot = acc_ref[rs, :] + o
            other = 1 - direction
            asig_o = _sigmoid(a0_ref[other] + lora(ad, wa_ref, other))
            kmod_o = k * (1.0 + (asig_o - 1.0) * ka_ref[...])
            gd = y[:, 3 * RW_W + 2 * LANES:]
            gate = _dot3(_sigmoid(gd), wg_ref[...])
            inv_n = 1.0 / RW_DH
            mean = seg_sum(tot) * inv_n
            xc = tot - mean
            var = seg_sum(xc * xc) * inv_n
            yn = xc * lax.rsqrt(var + RW_GN_EPS) * gnw_ref[...] + gnb_ref[...]
            bonus = seg_sum(r * (kmod + kmod_o) * rk_ref[...]) * v
            o_ref[0, rs, :] = (yn + bonus) * gate

    _run_two_way(chunk, 0, n_ctx)
    _run_two_way(chunk, n_ctx, n_all - n_ctx)


def _rwkv(pr, lc, mu, w0, wd, a0, wa, wg, k_k, k_a, r_k, gn_w, gn_b):
    b, t, _ = pr.shape
    kern = functools.partial(_rw_kernel, n_ctx=lc // CHUNK, n_all=t // CHUNK)
    consts = (mu, w0, wd, a0, wa, wg, k_k, k_a, r_k, gn_w, gn_b)
    return pl.pallas_call(
        kern,
        grid=(b,),
        in_specs=[pl.BlockSpec((1, t, RW_COLS_P), lambda i: (i, 0, 0))] + [_const_spec(a.shape) for a in consts],
        out_specs=pl.BlockSpec((1, t, RW_W), lambda i: (i, 0, 0)),
        out_shape=jax.ShapeDtypeStruct((b, t, RW_W), F32),
        scratch_shapes=[pltpu.VMEM((t, RW_W), F32),
                        pltpu.VMEM((2, RW_W // LANES, LANES, LANES), F32)],
        compiler_params=_params("arbitrary"),
        name="rwkv7",
    )(pr, *consts)


def _outproj_kernel(x_ref, a_ref, n_ref, c_ref, mod_ref, w_ref, lw_ref, lb_ref, o_ref, *, alpha):
    y = (_dot(_bf(a_ref[0]), w_ref[0:GLA_V, :])
         + _dot(_bf(n_ref[0]), w_ref[GLA_V:GLA_V + NAT_W, :])
         + _dot(_bf(c_ref[0]), w_ref[GLA_V + NAT_W:, :]))
    gate = mod_ref[0, 0, 2:3, :]
    o_ref[0] = _layer_norm(alpha * x_ref[0] + gate * y, lw_ref[...], lb_ref[...])


def _ffn_kernel(x_ref, mod_ref, w13_ref, w2_ref, lw_ref, lb_ref, o_ref, *, alpha, hidden, n_split):
    x = x_ref[0]
    h = _bf(x * (1.0 + mod_ref[0, 0, 4:5, :]) + mod_ref[0, 0, 3:4, :])
    hs = hidden // n_split
    f = None
    for i in range(n_split):
        gte = _dot(h, w13_ref[:, i * hs:(i + 1) * hs])
        up = _dot(h, w13_ref[:, hidden + i * hs:hidden + (i + 1) * hs])
        part = _dot(_bf(_silu(gte) * up), w2_ref[i * hs:(i + 1) * hs, :])
        f = part if f is None else f + part
    o_ref[0] = _layer_norm(alpha * x + mod_ref[0, 0, 5:6, :] * f, lw_ref[...], lb_ref[...])


def _token_specs(d, tm, nct, row_off):
    row = lambda bi, j: (bi, j + row_off, 0)
    seg = lambda bi, j: (bi, jnp.where(j + row_off >= nct, 1, 0), 0, 0)
    return row, seg, pl.BlockSpec((1, 1, 6, d), seg)


def _outproj(xz, ya, yn, yc, modtab, w_out, ln_w, ln_b, tm, nct, row_off, alpha):
    b, t, d = xz.shape
    t_out = t - row_off * tm
    row, _, mod_spec = _token_specs(d, tm, nct, row_off)
    return pl.pallas_call(
        functools.partial(_outproj_kernel, alpha=alpha),
        grid=(b, t_out // tm),
        in_specs=[pl.BlockSpec((1, tm, d), row),
                  pl.BlockSpec((1, tm, GLA_V), row),
                  pl.BlockSpec((1, tm, NAT_W), row),
                  pl.BlockSpec((1, tm, RW_W), row),
                  mod_spec, _const_spec(w_out.shape), _const_spec(ln_w.shape), _const_spec(ln_b.shape)],
        out_specs=pl.BlockSpec((1, tm, d), lambda bi, j: (bi, j, 0)),
        out_shape=jax.ShapeDtypeStruct((b, t_out, d), F32),
        compiler_params=_params("arbitrary", "arbitrary"),
        name="outproj_ln",
    )(xz, ya, yn, yc, modtab, w_out, ln_w, ln_b)


def _ffn(xz, modtab, w13, w2, ln_w, ln_b, tm, nct, row_off, alpha):
    b, t, d = xz.shape
    hidden = w2.shape[0]
    seg = lambda bi, j: (bi, jnp.where(j + row_off >= nct, 1, 0), 0, 0)
    row = lambda bi, j: (bi, j, 0)
    return pl.pallas_call(
        functools.partial(_ffn_kernel, alpha=alpha, hidden=hidden, n_split=2),
        grid=(b, t // tm),
        in_specs=[pl.BlockSpec((1, tm, d), row),
                  pl.BlockSpec((1, 1, 6, d), seg),
                  _const_spec(w13.shape), _const_spec(w2.shape),
                  _const_spec(ln_w.shape), _const_spec(ln_b.shape)],
        out_specs=pl.BlockSpec((1, tm, d), row),
        out_shape=jax.ShapeDtypeStruct((b, t, d), F32),
        compiler_params=_params("arbitrary", "arbitrary"),
        name="swiglu_ln",
    )(xz, modtab, w13, w2, ln_w, ln_b)


def _rope_tables(lc, l):
    t = jnp.arange(l, dtype=jnp.int32)
    row = (t // GRID_W).astype(F32)
    col = (t % GRID_W).astype(F32)
    n_freq = GLA_DK // 4
    inv = ROPE_BASE ** (-jnp.arange(n_freq, dtype=F32) / n_freq)
    ang = jnp.concatenate([row[:, None] * inv, col[:, None] * inv], axis=-1)
    pair = (np.arange(LANES) % GLA_DK) // 2
    sign = np.where(np.arange(LANES) % 2 == 0, -1.0, 1.0).astype(np.float32)
    cos_l = jnp.cos(ang)[:, pair]
    sin_l = jnp.sin(ang)[:, pair] * sign
    cos_t = jnp.concatenate([jnp.ones((lc, LANES), F32), cos_l], axis=0)
    sin_t = jnp.concatenate([jnp.zeros((lc, LANES), F32), sin_l], axis=0)
    return cos_t, sin_t


def _nat_bias_table(rpb):
    KR, KC, W = NAT_WIN_R, NAT_WIN_C, GRID_W
    cq = np.arange(W)
    col0 = np.clip(cq - KC // 2, 0, W - KC)
    in_win = (cq[None, :] >= col0[:, None]) & (cq[None, :] < col0[:, None] + KC)
    dc = np.clip(cq[None, :] - cq[:, None], -(KC - 1), KC - 1) + KC - 1
    delta = np.arange(KR)
    dr = np.arange(KR)[None, :] - delta[:, None] + NAT_WIN_R - 1
    tab = rpb[:, dr][:, :, :, dc]
    tab = jnp.where(in_win[None, None, None], tab, -jnp.inf)
    tab = tab.transpose(0, 1, 3, 2, 4)
    return tab.reshape(NAT_HEADS, KR, W, KR * W)


def _pad_cols(w, width):
    return jnp.pad(w, ((0, 0), (0, width - w.shape[1])))


def _layer_params(i, w_in, gla_gate_up, gla_gate_b, gla_norm_w, nat_rpb, rw_mu, rw_w0, rw_wd2, rw_a0, rw_wa2,
                  rw_wg2, rw_k_k, rw_k_a, rw_r_k, rw_gn_w, rw_gn_b, w_out, ln1_w, ln1_b, ffn_w13, ffn_w2,
                  ln2_w, ln2_b):
    wi = w_in[i]
    o = 0
    pieces = []
    for size, width in ((GLA_QK, GLA_QK_PAD), (GLA_QK, GLA_QK_PAD), (GLA_V, GLA_V), (GLA_V, GLA_V),
                        (2 * GLA_GATE_RANK, LANES)):
        pieces.append(_pad_cols(wi[:, o:o + size], width))
        o += size
    pieces.append(wi[:, o:])
    p = {"w_in": _bf(jnp.concatenate(pieces, axis=1))}

    wg = jnp.zeros((LANES, 2 * GLA_QK_PAD), F32)
    wg = wg.at[0:GLA_GATE_RANK, 0:GLA_QK].set(gla_gate_up[i, 0])
    wg = wg.at[GLA_GATE_RANK:2 * GLA_GATE_RANK, GLA_QK_PAD:GLA_QK_PAD + GLA_QK].set(gla_gate_up[i, 1])
    p["gla_wg"] = wg
    p["gla_gb"] = jnp.concatenate([_pad_cols(gla_gate_b[i, 0:1], GLA_QK_PAD),
                                   _pad_cols(gla_gate_b[i, 1:2], GLA_QK_PAD)], axis=1)
    p["gla_nw"] = jnp.tile(gla_norm_w[i], GLA_HEADS).reshape(1, GLA_V)
    p["nat_bias"] = _nat_bias_table(nat_rpb[i])

    def lora_pair(w, rank):
        z = jnp.zeros((2, LANES, RW_W), F32)
        z = z.at[0, 0:rank].set(w[0])
        return z.at[1, rank:2 * rank].set(w[1])

    row = lambda a: a.reshape(1, -1)
    p["rw"] = (row(rw_mu[i]), rw_w0[i].reshape(2, 1, RW_W), lora_pair(rw_wd2[i], RW_DECAY_RANK),
               rw_a0[i].reshape(2, 1, RW_W), lora_pair(rw_wa2[i], RW_A_RANK), rw_wg2[i],
               row(rw_k_k[i]), row(rw_k_a[i]), row(rw_r_k[i]), row(rw_gn_w[i]), row(rw_gn_b[i]))
    p["w_out"] = _bf(w_out[i])
    p["ln1"] = (row(ln1_w[i]), row(ln1_b[i]))
    p["w13"] = _bf(ffn_w13[i])
    p["w2"] = _bf(ffn_w2[i])
    p["ln2"] = (row(ln2_w[i]), row(ln2_b[i]))
    return p


def kernel(x, c, ctx, c_ctx, w_mod, b_mod, w_in, gla_gate_up, gla_gate_b, gla_norm_w, nat_rpb, rw_mu, rw_w0,
           rw_wd2, rw_a0, rw_wa2, rw_wg2, rw_k_k, rw_k_a, rw_r_k, rw_gn_w, rw_gn_b, w_out, ln1_w, ln1_b,
           ffn_w13, ffn_w2, ln2_w, ln2_b):
    b, l, d = x.shape
    lc = ctx.shape[1]
    depth = w_mod.shape[0]
    alpha = (2.0 * depth) ** 0.25
    tm = 256 if lc % 256 == 0 else 128
    assert lc % tm == 0 and l % tm == 0 and lc % (2 * CHUNK) == 0 and l % (2 * CHUNK) == 0
    assert l % GRID_W == 0 and l // GRID_W >= NAT_WIN_R
    nct = lc // tm

    cos_t, sin_t = _rope_tables(lc, l)
    rows = -(-(b + 1) // 8) * 8
    cvec = jnp.concatenate([c, c_ctx[None], jnp.zeros((rows - b - 1, d), F32)], axis=0)
    xz = jnp.concatenate([ctx, x], axis=1)

    for i in range(depth):
        last = i == depth - 1
        p = _layer_params(i, w_in, gla_gate_up, gla_gate_b, gla_norm_w, nat_rpb, rw_mu, rw_w0, rw_wd2, rw_a0,
                          rw_wa2, rw_wg2, rw_k_k, rw_k_a, rw_r_k, rw_gn_w, rw_gn_b, w_out, ln1_w, ln1_b,
                          ffn_w13, ffn_w2, ln2_w, ln2_b)
        m = _modulation(cvec, w_mod[i], b_mod[i])
        ml = m[:b].reshape(b, 1, 6, d)
        mc = jnp.broadcast_to(m[b].reshape(1, 1, 6, d), (b, 1, 6, d))
        modtab = jnp.concatenate([mc, ml], axis=1)

        pg, pn, pr = _inproj(xz, modtab, p["w_in"], tm, nct)
        ya = _gla(pg, cos_t, sin_t, p["gla_wg"], p["gla_gb"], p["gla_nw"], lc)
        yn = _nat(pn, p["nat_bias"], lc)
        yc = _rwkv(pr, lc, *p["rw"])
        row_off = nct if last else 0
        x1 = _outproj(xz, ya, yn, yc, modtab, p["w_out"], *p["ln1"], tm, nct, row_off, alpha)
        xz = _ffn(x1, modtab, p["w13"], p["w2"], *p["ln2"], tm, nct, row_off, alpha)
    return xz
```

```python
import functools
import math

import jax
import jax.numpy as jnp
import numpy as np
from jax import lax
from jax.experimental import pallas as pl
from jax.experimental.pallas import tpu as pltpu

F32 = jnp.float32
BF16 = jnp.bfloat16

GRID_W = 64
GLA_HEADS, GLA_DK, GLA_DV, GLA_GATE_RANK, GLA_TAU = 6, 32, 64, 16, 16.0
NAT_HEADS, NAT_DH, NAT_WIN_R, NAT_WIN_C = 4, 64, 8, 16
RW_HEADS, RW_DH, RW_DECAY_RANK, RW_A_RANK, RW_GATE_RANK = 6, 64, 64, 64, 128
RW_GN_EPS = 64e-5
RW_DECAY_SCALE = math.exp(-0.5)
ROPE_BASE = 10000.0
LN_EPS = 1e-5

GLA_QK = GLA_HEADS * GLA_DK
GLA_V = GLA_HEADS * GLA_DV
NAT_W = NAT_HEADS * NAT_DH
RW_W = RW_HEADS * RW_DH

LANES = 128
CHUNK = 64
GLA_QK_PAD = 256
GLA_COLS_P = 2 * GLA_QK_PAD + 2 * GLA_V + LANES
NAT_COLS_P = 3 * NAT_W
RW_COLS_P = 3 * RW_W + 3 * LANES
VMEM_LIMIT = 56 * 1024 * 1024

NN = (((1,), (0,)), ((), ()))
NT = (((1,), (1,)), ((), ()))
TN = (((0,), (0,)), ((), ()))


def _dot(a, b, dims=NN):
    return lax.dot_general(a, b, dims, preferred_element_type=F32)


def _bf(x):
    return x.astype(BF16)


def _split(x):
    hi = x.astype(BF16)
    lo = (x - hi.astype(F32)).astype(BF16)
    return hi, lo


def _dot3(a, b, dims=NN):
    ah, al = _split(a)
    bh, bl = _split(b)
    return _dot(ah, bh, dims) + (_dot(ah, bl, dims) + _dot(al, bh, dims))


def _dot_exact_lhs(a_bf, b, dims=NN):
    bh, bl = _split(b)
    return _dot(a_bf, bh, dims) + _dot(a_bf, bl, dims)


def _dot_exact_rhs(a, b_bf, dims=NN):
    ah, al = _split(a)
    return _dot(ah, b_bf, dims) + _dot(al, b_bf, dims)


def _sigmoid(x):
    return 1.0 / (1.0 + jnp.exp(-x))


def _silu(x):
    return x * _sigmoid(x)


def _iota(shape, axis):
    return lax.broadcasted_iota(jnp.int32, shape, axis)


def _div_pow2(x, n):
    return lax.shift_right_logical(x, int(math.log2(n)))


def _mod_pow2(x, n):
    return lax.bitwise_and(x, n - 1)


def _layer_norm(h, w, b):
    mu = jnp.mean(h, axis=-1, keepdims=True)
    xc = h - mu
    var = jnp.mean(xc * xc, axis=-1, keepdims=True)
    return xc * lax.rsqrt(var + LN_EPS) * w + b


def _params(*sem):
    return pltpu.CompilerParams(dimension_semantics=sem, vmem_limit_bytes=VMEM_LIMIT)


def _const_spec(shape):
    nd = len(shape)
    return pl.BlockSpec(shape, lambda *_: (0,) * nd)


def _mod_kernel(c_ref, w_ref, b_ref, o_ref):
    o_ref[...] = _dot3(_silu(c_ref[...]), w_ref[...]) + b_ref[...]


def _modulation(cvec, w_mod, b_mod):
    rows, d = cvec.shape
    n = w_mod.shape[1]
    tn = 1024
    return pl.pallas_call(
        _mod_kernel,
        grid=(n // tn,),
        in_specs=[pl.BlockSpec((rows, d), lambda j: (0, 0)),
                  pl.BlockSpec((d, tn), lambda j: (0, j)),
                  pl.BlockSpec((1, tn), lambda j: (0, j))],
        out_specs=pl.BlockSpec((rows, tn), lambda j: (0, j)),
        out_shape=jax.ShapeDtypeStruct((rows, n), F32),
        compiler_params=_params("arbitrary"),
        name="modulation",
    )(cvec, w_mod, b_mod.reshape(1, n))


def _inproj_kernel(x_ref, mod_ref, w_ref, og_ref, on_ref, or_ref):
    x = x_ref[0]
    sh = mod_ref[0, 0, 0:1, :]
    sc = mod_ref[0, 0, 1:2, :]
    xm = _bf(x * (1.0 + sc) + sh)
    c0, c1 = GLA_COLS_P, GLA_COLS_P + NAT_COLS_P
    og_ref[0] = _dot(xm, w_ref[:, 0:c0])
    on_ref[0] = _dot(xm, w_ref[:, c0:c1])
    or_ref[0] = _dot(xm, w_ref[:, c1:])


def _inproj(xz, modtab, w_in_p, tm, nct):
    b, t, d = xz.shape
    ncols = w_in_p.shape[1]
    seg = lambda bi, j: (bi, jnp.where(j >= nct, 1, 0), 0, 0)
    row = lambda bi, j: (bi, j, 0)
    return pl.pallas_call(
        _inproj_kernel,
        grid=(b, t // tm),
        in_specs=[pl.BlockSpec((1, tm, d), row),
                  pl.BlockSpec((1, 1, 6, d), seg),
                  _const_spec((d, ncols))],
        out_specs=[pl.BlockSpec((1, tm, GLA_COLS_P), row),
                   pl.BlockSpec((1, tm, NAT_COLS_P), row),
                   pl.BlockSpec((1, tm, RW_COLS_P), row)],
        out_shape=[jax.ShapeDtypeStruct((b, t, GLA_COLS_P), F32),
                   jax.ShapeDtypeStruct((b, t, NAT_COLS_P), F32),
                   jax.ShapeDtypeStruct((b, t, RW_COLS_P), F32)],
        compiler_params=_params("arbitrary", "arbitrary"),
        name="inproj",
    )(xz, modtab, w_in_p)


def _tri_ones(direction):
    r = _iota((CHUNK, CHUNK), 0)
    c = _iota((CHUNK, CHUNK), 1)
    keep = (c <= r) if direction == 0 else (c >= r)
    return jnp.where(keep, 1.0, 0.0).astype(BF16)


def _block_ones(width, block):
    r = _div_pow2(_iota((width, width), 0), block)
    c = _div_pow2(_iota((width, width), 1), block)
    return jnp.where(r == c, 1.0, 0.0).astype(BF16)


def _stages(streams):
    streams = list(streams)
    while streams:
        alive = []
        for s in streams:
            try:
                next(s)
                alive.append(s)
            except StopIteration:
                pass
        streams = alive
        yield


def _lockstep(streams):
    for _ in _stages(streams):
        pass


def _run_two_way(chunk_fn, c_lo, n):
    half = n // 2

    def first(i, carry):
        _lockstep([chunk_fn(c_lo + i, 0, True, c_lo, n), chunk_fn(c_lo + n - 1 - i, 1, True, c_lo, n)])
        return carry

    def second(i, carry):
        _lockstep([chunk_fn(c_lo + half + i, 0, False, c_lo, n),
                   chunk_fn(c_lo + half - 1 - i, 1, False, c_lo, n)])
        return carry

    lax.fori_loop(0, half, first, 0)
    lax.fori_loop(0, half, second, 0)


def _gla_kernel(pg_ref, cos_ref, sin_ref, wg_ref, gb_ref, nw_ref, o_ref, acc_ref, s_ref, *, n_ctx, n_all):
    C = CHUNK
    s_ref[...] = jnp.zeros(s_ref.shape, F32)

    lane_k = _iota((1, GLA_QK_PAD), 1)
    lane_v = _iota((1, GLA_V), 1)
    head_k = [_div_pow2(lane_k, GLA_DK) == h for h in range(GLA_HEADS)]
    head_v = [_div_pow2(lane_v, GLA_DV) == h for h in range(GLA_HEADS)]
    even_lane = _mod_pow2(_iota((1, LANES), 1), 2) == 0
    rows = _iota((C, GLA_HEADS * C), 0)
    cols = _mod_pow2(_iota((C, GLA_HEADS * C), 1), C)
    causal = [cols <= rows, cols >= rows]
    tri = [_tri_ones(0), _tri_ones(1)]
    bd_v = _block_ones(GLA_V, GLA_DV)
    s_rows = _div_pow2(_iota((GLA_V, GLA_QK_PAD), 0), GLA_DV)
    s_cols = _div_pow2(_iota((GLA_V, GLA_QK_PAD), 1), GLA_DK)
    s_diag = s_rows == s_cols
    scale = GLA_DK ** -0.5

    def chunk(c, direction, is_first, c_lo, n):
        del c_lo, n
        r0 = pl.multiple_of(c * C, C)
        rs = pl.ds(r0, C)
        cs = cos_ref[rs, :]
        sn = sin_ref[rs, :]

        def rope(col0):
            parts = []
            for j in range(GLA_QK_PAD // LANES):
                xb = pg_ref[0, rs, col0 + j * LANES:col0 + (j + 1) * LANES]
                sw = jnp.where(even_lane, pltpu.roll(xb, LANES - 1, 1), pltpu.roll(xb, 1, 1))
                parts.append(xb * cs + sw * sn)
            return jnp.concatenate(parts, axis=1)

        q = rope(0)
        k = rope(GLA_QK_PAD)
        v = pg_ref[0, rs, 2 * GLA_QK_PAD:2 * GLA_QK_PAD + GLA_V]
        dn = pg_ref[0, rs, 2 * GLA_QK_PAD + 2 * GLA_V:]
        wcols = slice(direction * GLA_QK_PAD, (direction + 1) * GLA_QK_PAD)
        z = _dot3(dn, wg_ref[:, wcols]) + gb_ref[:, wcols]
        yield
        logd = jax.nn.log_sigmoid(z) * (1.0 / GLA_TAU)
        bcum = _dot_exact_lhs(tri[direction], logd)
        yield
        b_end = bcum[C - 1:C, :] if direction == 0 else bcum[0:1, :]
        q_e = _bf(q * jnp.exp(bcum) * scale)
        k_e = k * jnp.exp(-bcum)
        k_l = _bf(k * jnp.exp(b_end - bcum))

        k_stack = jnp.concatenate([_bf(jnp.where(head_k[h], k_e, 0.0)) for h in range(GLA_HEADS)], axis=0)
        att = _dot(q_e, k_stack, NT)
        state = s_ref[direction]
        o_inter = _dot(q_e, _bf(state), NT)
        kv = _dot(_bf(v), k_l, TN)
        yield
        att = _bf(jnp.where(causal[direction], att, 0.0))
        v_bd = jnp.concatenate([_bf(jnp.where(head_v[h], v, 0.0)) for h in range(GLA_HEADS)], axis=0)
        o = _dot(att, v_bd) + o_inter
        s_ref[direction] = state * jnp.exp(b_end) + jnp.where(s_diag, kv, 0.0)
        yield

        if is_first:
            acc_ref[rs, :] = o
        else:
            tot = acc_ref[rs, :] + o
            g = pg_ref[0, rs, 2 * GLA_QK_PAD + GLA_V:2 * GLA_QK_PAD + 2 * GLA_V]
            ms = _dot_exact_rhs(tot * tot, bd_v) * (1.0 / GLA_DV)
            yield
            o_ref[0, rs, :] = tot * lax.rsqrt(ms + LN_EPS) * nw_ref[...] * _silu(g)

    _run_two_way(chunk, 0, n_ctx)
    _run_two_way(chunk, n_ctx, n_all - n_ctx)


def _gla(pg, cos_t, sin_t, wg, gb, nw, lc):
    b, t, _ = pg.shape
    kern = functools.partial(_gla_kernel, n_ctx=lc // CHUNK, n_all=t // CHUNK)
    return pl.pallas_call(
        kern,
        grid=(b,),
        in_specs=[pl.BlockSpec((1, t, GLA_COLS_P), lambda i: (i, 0, 0)),
                  _const_spec((t, LANES)), _const_spec((t, LANES)),
                  _const_spec(wg.shape), _const_spec(gb.shape), _const_spec(nw.shape)],
        out_specs=pl.BlockSpec((1, t, GLA_V), lambda i: (i, 0, 0)),
        out_shape=jax.ShapeDtypeStruct((b, t, GLA_V), F32),
        scratch_shapes=[pltpu.VMEM((t, GLA_V), F32),
                        pltpu.VMEM((2, GLA_V, GLA_QK_PAD), F32)],
        compiler_params=_params("arbitrary"),
        name="gla",
    )(pg, cos_t, sin_t, wg, gb, nw)


def _nat_kernel(pn_ref, bias_ref, o_ref, *, lc, n_rows):
    W = GRID_W
    KR = NAT_WIN_R
    scale = NAT_DH ** -0.5
    half = [_iota((1, LANES), 1) < NAT_DH, _iota((1, LANES), 1) >= NAT_DH]
    n_pairs = NAT_W // LANES

    def col(group, j):
        return slice(group * NAT_W + j * LANES, group * NAT_W + (j + 1) * LANES)

    def pair_stream(j, q_rows, key_rows, bias_of_head):
        nq = q_rows.size
        q = pn_ref[0, q_rows, col(0, j)] * scale
        qm = _bf(jnp.concatenate([jnp.where(half[0], q, 0.0), jnp.where(half[1], q, 0.0)], axis=0))
        keys = [_bf(pn_ref[0, 0:lc, col(1, j)])]
        vals = [_bf(pn_ref[0, 0:lc, col(2, j)])]
        if key_rows is not None:
            keys.insert(0, _bf(pn_ref[0, key_rows, col(1, j)]))
            vals.insert(0, _bf(pn_ref[0, key_rows, col(2, j)]))
        scores = [_dot(qm, kk, NT) for kk in keys]
        yield
        if key_rows is not None:
            scores[0] = scores[0] + jnp.concatenate([bias_of_head(2 * j), bias_of_head(2 * j + 1)], axis=0)
        m = scores[0].max(axis=-1, keepdims=True)
        for s in scores[1:]:
            m = jnp.maximum(m, s.max(axis=-1, keepdims=True))
        ps = [jnp.exp(s - m) for s in scores]
        l = ps[0].sum(axis=-1, keepdims=True)
        for p in ps[1:]:
            l = l + p.sum(axis=-1, keepdims=True)
        o = _dot(_bf(ps[0]), vals[0])
        for p, vv in zip(ps[1:], vals[1:]):
            o = o + _dot(_bf(p), vv)
        yield
        o = o / l
        o_ref[0, q_rows, col(0, j)] = jnp.where(half[0], o[0:nq], o[nq:])

    _lockstep([pair_stream(j, pl.ds(0, lc), None, None) for j in range(n_pairs)])

    def row_streams(r):
        start = jnp.clip(r - KR // 2, 0, n_rows - KR)
        delta = r - start
        q0 = pl.multiple_of(lc + r * W, W)
        k0 = pl.multiple_of(lc + start * W, W)
        return [pair_stream(j, pl.ds(q0, W), pl.ds(k0, KR * W), lambda h: bias_ref[h, delta])
                for j in range(n_pairs)]

    def row_body(i, carry):
        _lockstep(row_streams(2 * i) + row_streams(2 * i + 1))
        return carry

    lax.fori_loop(0, n_rows // 2, row_body, 0)


def _nat(pn, bias_tab, lc):
    b, t, _ = pn.shape
    kern = functools.partial(_nat_kernel, lc=lc, n_rows=(t - lc) // GRID_W)
    return pl.pallas_call(
        kern,
        grid=(b,),
        in_specs=[pl.BlockSpec((1, t, NAT_COLS_P), lambda i: (i, 0, 0)),
                  _const_spec(bias_tab.shape)],
        out_specs=pl.BlockSpec((1, t, NAT_W), lambda i: (i, 0, 0)),
        out_shape=jax.ShapeDtypeStruct((b, t, NAT_W), F32),
        compiler_params=_params("arbitrary"),
        name="nat",
    )(pn, bias_tab)


def _rw_kernel(pr_ref, mu_ref, w0_ref, wd_ref, a0_ref, wa_ref, wg_ref, kk_ref, ka_ref, rk_ref,
               gnw_ref, gnb_ref, o_ref, acc_ref, s_ref, *, n_ctx, n_all):
    C = CHUNK
    C2 = 2 * C
    n_pairs = RW_W // LANES
    t_all = n_all * C
    s_ref[...] = jnp.zeros(s_ref.shape, F32)

    lane = _iota((1, LANES), 1)
    half = [lane < RW_DH, lane >= RW_DH]
    row_id = _iota((C, 1), 0)
    tri = [_tri_ones(0), _tri_ones(1)]
    bd = _block_ones(RW_W, RW_DH)
    r_c = _iota((C, C2), 0)
    c_c = _mod_pow2(_iota((C, C2), 1), C)
    incl = [c_c <= r_c, c_c >= r_c]
    r_s = _iota((C2, C2), 0)
    c_s = _iota((C2, C2), 1)
    strict = [c_s < r_s, c_s > r_s]
    n_double = int(math.log2(C))

    def seg_sum(x):
        return _dot_exact_rhs(x, bd)

    def lora(x, w_ref, direction):
        return _dot3(x, w_ref[direction])

    def chunk(c, direction, is_first, c_lo, n):
        r0 = pl.multiple_of(c * C, C)
        rs = pl.ds(r0, C)
        p = pr_ref[0, rs, :]
        prev_row = pr_ref[0, pl.ds(pl.multiple_of(jnp.maximum(r0 - 8, 0), 8), 8), :][7:8, :]
        next_row = pr_ref[0, pl.ds(pl.multiple_of(jnp.minimum(r0 + C, t_all - 8), 8), 8), :][0:1, :]
        prev_row = jnp.where(c == c_lo, 0.0, prev_row)
        next_row = jnp.where(c == c_lo + n - 1, 0.0, next_row)
        up = jnp.where(row_id == 0, prev_row, pltpu.roll(p, 1, 0))
        dn = jnp.where(row_id == C - 1, next_row, pltpu.roll(p, C - 1, 0))
        y = p + (0.5 * (up + dn) - p) * mu_ref[...]

        r = y[:, 0:RW_W]
        k = y[:, RW_W:2 * RW_W]
        v = y[:, 2 * RW_W:3 * RW_W]
        dd = y[:, 3 * RW_W:3 * RW_W + LANES]
        ad = y[:, 3 * RW_W + LANES:3 * RW_W + 2 * LANES]

        other = 1 - direction
        d_lin = lora(jnp.tanh(dd), wd_ref, direction)
        a_lin = lora(ad, wa_ref, direction)
        kk = k * kk_ref[...]
        kk_ss = seg_sum(kk * kk)
        if not is_first:
            a_lin_o = lora(ad, wa_ref, other)
            gate = _dot3(_sigmoid(y[:, 3 * RW_W + 2 * LANES:]), wg_ref[...])
        yield
        logw = -RW_DECAY_SCALE * _sigmoid(w0_ref[direction] + d_lin)
        asig = _sigmoid(a0_ref[direction] + a_lin)
        kk = kk * lax.rsqrt(kk_ss + 1e-12)
        kmod = k * (1.0 + (asig - 1.0) * ka_ref[...])
        gcum = _dot_exact_lhs(tri[direction], logw)
        if not is_first:
            kmod_o = k * (1.0 + (_sigmoid(a0_ref[other] + a_lin_o) - 1.0) * ka_ref[...])
            bonus_w = seg_sum(r * (kmod + kmod_o) * rk_ref[...])
        yield
        g_end = gcum[C - 1:C, :] if direction == 0 else gcum[0:1, :]
        e_pos = jnp.exp(gcum)
        e_neg = jnp.exp(-gcum)
        r_t = r * e_pos
        a_t = -kk * jnp.exp(gcum - logw)
        b_t = kk * asig * e_neg
        k_t = kmod * e_neg
        decay_end = jnp.exp(g_end)

        outs = [None] * n_pairs

        def pair(j):
            cl = slice(j * LANES, (j + 1) * LANES)

            def stack(x):
                xb = x[:, cl]
                return jnp.concatenate([jnp.where(half[0], xb, 0.0), jnp.where(half[1], xb, 0.0)], axis=0)

            a_st = stack(a_t)
            v_st = _bf(stack(v))
            rhs = _bf(jnp.concatenate([stack(b_t), stack(k_t)], axis=0))
            lhs = _bf(jnp.concatenate([r_t[:, cl], a_st], axis=0))
            sc = _dot(lhs, rhs, NT)
            yield
            a_rb = jnp.where(incl[direction], sc[0:C, 0:C2], 0.0)
            a_rk = jnp.where(incl[direction], sc[0:C, C2:], 0.0)
            n_bd = jnp.where(strict[direction], sc[C:, 0:C2], 0.0)
            a_ak = jnp.where(strict[direction], sc[C:, C2:], 0.0)
            y2 = _dot(_bf(a_ak), v_st)
            npow = _bf(n_bd)
            npow2 = _dot(npow, npow)
            yield
            x = jnp.concatenate([a_st, y2], axis=1)
            for i in range(n_double):
                upd = _dot(npow, _bf(x))
                if i + 1 < n_double:
                    npow = _bf(npow2)
                    if i + 2 < n_double:
                        npow2 = _dot(npow, npow)
                yield
                x = x + upd
            a_hat = x[:, 0:LANES]
            u_hat = x[:, LANES:]
            state = s_ref[direction, j]
            m3 = _dot(_bf(jnp.concatenate([r_t[:, cl], a_hat], axis=0)), _bf(state), NT)
            yield
            u = m3[C:] + u_hat
            uv = jnp.concatenate([_bf(u), v_st], axis=0)
            o_j = _dot(_bf(jnp.concatenate([a_rb, a_rk], axis=1)), uv)
            s_new = _dot(uv, rhs, TN)
            yield
            outs[j] = m3[0:C] + o_j
            s_ref[direction, j] = (state + s_new) * decay_end[:, cl]

        yield from _stages([pair(j) for j in range(n_pairs)])
        o = jnp.concatenate(outs, axis=1)

        if is_first:
            acc_ref[rs, :] = o
        else:
            tot = acc_ref[rs, :] + o
            inv_n = 1.0 / RW_DH
            mean = seg_sum(tot) * inv_n
            yield
            xc = tot - mean
            var = seg_sum(xc * xc) * inv_n
            yield
            yn = xc * lax.rsqrt(var + RW_GN_EPS) * gnw_ref[...] + gnb_ref[...]
            o_ref[0, rs, :] = (yn + bonus_w * v) * gate

    _run_two_way(chunk, 0, n_ctx)
    _run_two_way(chunk, n_ctx, n_all - n_ctx)


def _rwkv(pr, lc, mu, w0, wd, a0, wa, wg, k_k, k_a, r_k, gn_w, gn_b):
    b, t, _ = pr.shape
    kern = functools.partial(_rw_kernel, n_ctx=lc // CHUNK, n_all=t // CHUNK)
    consts = (mu, w0, wd, a0, wa, wg, k_k, k_a, r_k, gn_w, gn_b)
    return pl.pallas_call(
        kern,
        grid=(b,),
        in_specs=[pl.BlockSpec((1, t, RW_COLS_P), lambda i: (i, 0, 0))] + [_const_spec(a.shape) for a in consts],
        out_specs=pl.BlockSpec((1, t, RW_W), lambda i: (i, 0, 0)),
        out_shape=jax.ShapeDtypeStruct((b, t, RW_W), F32),
        scratch_shapes=[pltpu.VMEM((t, RW_W), F32),
                        pltpu.VMEM((2, RW_W // LANES, LANES, LANES), F32)],
        compiler_params=_params("arbitrary"),
        name="rwkv7",
    )(pr, *consts)


def _outproj_kernel(x_ref, a_ref, n_ref, c_ref, mod_ref, w_ref, lw_ref, lb_ref, o_ref, *, alpha):
    y = (_dot(_bf(a_ref[0]), w_ref[0:GLA_V, :])
         + _dot(_bf(n_ref[0]), w_ref[GLA_V:GLA_V + NAT_W, :])
         + _dot(_bf(c_ref[0]), w_ref[GLA_V + NAT_W:, :]))
    gate = mod_ref[0, 0, 2:3, :]
    o_ref[0] = _layer_norm(alpha * x_ref[0] + gate * y, lw_ref[...], lb_ref[...])


def _ffn_kernel(x_ref, mod_ref, w13_ref, w2_ref, lw_ref, lb_ref, o_ref, *, alpha, hidden, n_split):
    x = x_ref[0]
    h = _bf(x * (1.0 + mod_ref[0, 0, 4:5, :]) + mod_ref[0, 0, 3:4, :])
    hs = hidden // n_split
    f = None
    for i in range(n_split):
        gte = _dot(h, w13_ref[:, i * hs:(i + 1) * hs])
        up = _dot(h, w13_ref[:, hidden + i * hs:hidden + (i + 1) * hs])
        part = _dot(_bf(_silu(gte) * up), w2_ref[i * hs:(i + 1) * hs, :])
        f = part if f is None else f + part
    o_ref[0] = _layer_norm(alpha * x + mod_ref[0, 0, 5:6, :] * f, lw_ref[...], lb_ref[...])


def _token_specs(d, tm, nct, row_off):
    row = lambda bi, j: (bi, j + row_off, 0)
    seg = lambda bi, j: (bi, jnp.where(j + row_off >= nct, 1, 0), 0, 0)
    return row, seg, pl.BlockSpec((1, 1, 6, d), seg)


def _outproj(xz, ya, yn, yc, modtab, w_out, ln_w, ln_b, tm, nct, row_off, alpha):
    b, t, d = xz.shape
    t_out = t - row_off * tm
    row, _, mod_spec = _token_specs(d, tm, nct, row_off)
    return pl.pallas_call(
        functools.partial(_outproj_kernel, alpha=alpha),
        grid=(b, t_out // tm),
        in_specs=[pl.BlockSpec((1, tm, d), row),
                  pl.BlockSpec((1, tm, GLA_V), row),
                  pl.BlockSpec((1, tm, NAT_W), row),
                  pl.BlockSpec((1, tm, RW_W), row),
                  mod_spec, _const_spec(w_out.shape), _const_spec(ln_w.shape), _const_spec(ln_b.shape)],
        out_specs=pl.BlockSpec((1, tm, d), lambda bi, j: (bi, j, 0)),
        out_shape=jax.ShapeDtypeStruct((b, t_out, d), F32),
        compiler_params=_params("arbitrary", "arbitrary"),
        name="outproj_ln",
    )(xz, ya, yn, yc, modtab, w_out, ln_w, ln_b)


def _ffn(xz, modtab, w13, w2, ln_w, ln_b, tm, nct, row_off, alpha):
    b, t, d = xz.shape
    hidden = w2.shape[0]
    seg = lambda bi, j: (bi, jnp.where(j + row_off >= nct, 1, 0), 0, 0)
    row = lambda bi, j: (bi, j, 0)
    return pl.pallas_call(
        functools.partial(_ffn_kernel, alpha=alpha, hidden=hidden, n_split=2),
        grid=(b, t // tm),
        in_specs=[pl.BlockSpec((1, tm, d), row),
                  pl.BlockSpec((1, 1, 6, d), seg),
                  _const_spec(w13.shape), _const_spec(w2.shape),
                  _const_spec(ln_w.shape), _const_spec(ln_b.shape)],
        out_specs=pl.BlockSpec((1, tm, d), row),
        out_shape=jax.ShapeDtypeStruct((b, t, d), F32),
        compiler_params=_params("arbitrary", "arbitrary"),
        name="swiglu_ln",
    )(xz, modtab, w13, w2, ln_w, ln_b)


def _rope_tables(lc, l):
    t = jnp.arange(l, dtype=jnp.int32)
    row = (t // GRID_W).astype(F32)
    col = (t % GRID_W).astype(F32)
    n_freq = GLA_DK // 4
    inv = ROPE_BASE ** (-jnp.arange(n_freq, dtype=F32) / n_freq)
    ang = jnp.concatenate([row[:, None] * inv, col[:, None] * inv], axis=-1)
    pair = (np.arange(LANES) % GLA_DK) // 2
    sign = np.where(np.arange(LANES) % 2 == 0, -1.0, 1.0).astype(np.float32)
    cos_l = jnp.cos(ang)[:, pair]
    sin_l = jnp.sin(ang)[:, pair] * sign
    cos_t = jnp.concatenate([jnp.ones((lc, LANES), F32), cos_l], axis=0)
    sin_t = jnp.concatenate([jnp.zeros((lc, LANES), F32), sin_l], axis=0)
    return cos_t, sin_t


def _nat_bias_table(rpb):
    KR, KC, W = NAT_WIN_R, NAT_WIN_C, GRID_W
    cq = np.arange(W)
    col0 = np.clip(cq - KC // 2, 0, W - KC)
    in_win = (cq[None, :] >= col0[:, None]) & (cq[None, :] < col0[:, None] + KC)
    dc = np.clip(cq[None, :] - cq[:, None], -(KC - 1), KC - 1) + KC - 1
    delta = np.arange(KR)
    dr = np.arange(KR)[None, :] - delta[:, None] + NAT_WIN_R - 1
    tab = rpb[:, dr][:, :, :, dc]
    tab = jnp.where(in_win[None, None, None], tab, -jnp.inf)
    tab = tab.transpose(0, 1, 3, 2, 4)
    return tab.reshape(NAT_HEADS, KR, W, KR * W)


def _pad_cols(w, width):
    return jnp.pad(w, ((0, 0), (0, width - w.shape[1])))


def _layer_params(i, w_in, gla_gate_up, gla_gate_b, gla_norm_w, nat_rpb, rw_mu, rw_w0, rw_wd2, rw_a0, rw_wa2,
                  rw_wg2, rw_k_k, rw_k_a, rw_r_k, rw_gn_w, rw_gn_b, w_out, ln1_w, ln1_b, ffn_w13, ffn_w2,
                  ln2_w, ln2_b):
    wi = w_in[i]
    o = 0
    pieces = []
    for size, width in ((GLA_QK, GLA_QK_PAD), (GLA_QK, GLA_QK_PAD), (GLA_V, GLA_V), (GLA_V, GLA_V),
                        (2 * GLA_GATE_RANK, LANES)):
        pieces.append(_pad_cols(wi[:, o:o + size], width))
        o += size
    pieces.append(wi[:, o:])
    p = {"w_in": _bf(jnp.concatenate(pieces, axis=1))}

    wg = jnp.zeros((LANES, 2 * GLA_QK_PAD), F32)
    wg = wg.at[0:GLA_GATE_RANK, 0:GLA_QK].set(gla_gate_up[i, 0])
    wg = wg.at[GLA_GATE_RANK:2 * GLA_GATE_RANK, GLA_QK_PAD:GLA_QK_PAD + GLA_QK].set(gla_gate_up[i, 1])
    p["gla_wg"] = wg
    p["gla_gb"] = jnp.concatenate([_pad_cols(gla_gate_b[i, 0:1], GLA_QK_PAD),
                                   _pad_cols(gla_gate_b[i, 1:2], GLA_QK_PAD)], axis=1)
    p["gla_nw"] = jnp.tile(gla_norm_w[i], GLA_HEADS).reshape(1, GLA_V)
    p["nat_bias"] = _nat_bias_table(nat_rpb[i])

    def lora_pair(w, rank):
        z = jnp.zeros((2, LANES, RW_W), F32)
        z = z.at[0, 0:rank].set(w[0])
        return z.at[1, rank:2 * rank].set(w[1])

    row = lambda a: a.reshape(1, -1)
    p["rw"] = (row(rw_mu[i]), rw_w0[i].reshape(2, 1, RW_W), lora_pair(rw_wd2[i], RW_DECAY_RANK),
               rw_a0[i].reshape(2, 1, RW_W), lora_pair(rw_wa2[i], RW_A_RANK), rw_wg2[i],
               row(rw_k_k[i]), row(rw_k_a[i]), row(rw_r_k[i]), row(rw_gn_w[i]), row(rw_gn_b[i]))
    p["w_out"] = _bf(w_out[i])
    p["ln1"] = (row(ln1_w[i]), row(ln1_b[i]))
    p["w13"] = _bf(ffn_w13[i])
    p["w2"] = _bf(ffn_w2[i])
    p["ln2"] = (row(ln2_w[i]), row(ln2_b[i]))
    return p


def kernel(x, c, ctx, c_ctx, w_mod, b_mod, w_in, gla_gate_up, gla_gate_b, gla_norm_w, nat_rpb, rw_mu, rw_w0,
           rw_wd2, rw_a0, rw_wa2, rw_wg2, rw_k_k, rw_k_a, rw_r_k, rw_gn_w, rw_gn_b, w_out, ln1_w, ln1_b,
           ffn_w13, ffn_w2, ln2_w, ln2_b):
    b, l, d = x.shape
    lc = ctx.shape[1]
    depth = w_mod.shape[0]
    alpha = (2.0 * depth) ** 0.25
    tm = 256 if lc % 256 == 0 else 128
    assert lc % tm == 0 and l % tm == 0 and lc % (2 * CHUNK) == 0 and l % (2 * CHUNK) == 0
    assert l % (2 * GRID_W) == 0 and l // GRID_W >= NAT_WIN_R
    nct = lc // tm

    cos_t, sin_t = _rope_tables(lc, l)
    rows = -(-(b + 1) // 8) * 8
    cvec = jnp.concatenate([c, c_ctx[None], jnp.zeros((rows - b - 1, d), F32)], axis=0)
    xz = jnp.concatenate([ctx, x], axis=1)

    for i in range(depth):
        last = i == depth - 1
        p = _layer_params(i, w_in, gla_gate_up, gla_gate_b, gla_norm_w, nat_rpb, rw_mu, rw_w0, rw_wd2, rw_a0,
                          rw_wa2, rw_wg2, rw_k_k, rw_k_a, rw_r_k, rw_gn_w, rw_gn_b, w_out, ln1_w, ln1_b,
                          ffn_w13, ffn_w2, ln2_w, ln2_b)
        m = _modulation(cvec, w_mod[i], b_mod[i])
        ml = m[:b].reshape(b, 1, 6, d)
        mc = jnp.broadcast_to(m[b].reshape(1, 1, 6, d), (b, 1, 6, d))
        modtab = jnp.concatenate([mc, ml], axis=1)

        pg, pn, pr = _inproj(xz, modtab, p["w_in"], tm, nct)
        ya = _gla(pg, cos_t, sin_t, p["gla_wg"], p["gla_gb"], p["gla_nw"], lc)
        yn = _nat(pn, p["nat_bias"], lc)
        yc = _rwkv(pr, lc, *p["rw"])
        row_off = nct if last else 0
        x1 = _outproj(xz, ya, yn, yc, modtab, p["w_out"], *p["ln1"], tm, nct, row_off, alpha)
        xz = _ffn(x1, modtab, p["w13"], p["w2"], *p["ln2"], tm, nct, row_off, alpha)
    return xz
```

```python
import functools
import math

import jax
import jax.numpy as jnp
import numpy as np
from jax import lax
from jax.experimental import pallas as pl
from jax.experimental.pallas import tpu as pltpu

F32 = jnp.float32
BF16 = jnp.bfloat16

GRID_W = 64
GLA_HEADS, GLA_DK, GLA_DV, GLA_GATE_RANK, GLA_TAU = 6, 32, 64, 16, 16.0
NAT_HEADS, NAT_DH, NAT_WIN_R, NAT_WIN_C = 4, 64, 8, 16
RW_HEADS, RW_DH, RW_DECAY_RANK, RW_A_RANK, RW_GATE_RANK = 6, 64, 64, 64, 128
RW_GN_EPS = 64e-5
RW_DECAY_SCALE = math.exp(-0.5)
ROPE_BASE = 10000.0
LN_EPS = 1e-5

GLA_QK = GLA_HEADS * GLA_DK
GLA_V = GLA_HEADS * GLA_DV
NAT_W = NAT_HEADS * NAT_DH
RW_W = RW_HEADS * RW_DH

LANES = 128
CHUNK = 64
GLA_QK_PAD = 256
GLA_COLS_P = 2 * GLA_QK_PAD + 2 * GLA_V + LANES
NAT_COLS_P = 3 * NAT_W
RW_COLS_P = 3 * RW_W + 3 * LANES
VMEM_LIMIT = 56 * 1024 * 1024
DENSE_BATCH = 2
MIXER_BATCH = 2
NAT_ROWS_PER_STEP = 4

NN = (((1,), (0,)), ((), ()))
NT = (((1,), (1,)), ((), ()))
TN = (((0,), (0,)), ((), ()))


def _dot(a, b, dims=NN):
    return lax.dot_general(a, b, dims, preferred_element_type=F32)


def _bf(x):
    return x.astype(BF16)


def _split(x):
    hi = x.astype(BF16)
    lo = (x - hi.astype(F32)).astype(BF16)
    return hi, lo


def _dot3(a, b, dims=NN):
    ah, al = _split(a)
    bh, bl = _split(b)
    return _dot(ah, bh, dims) + (_dot(ah, bl, dims) + _dot(al, bh, dims))


def _dot_exact_lhs(a_bf, b, dims=NN):
    bh, bl = _split(b)
    return _dot(a_bf, bh, dims) + _dot(a_bf, bl, dims)


def _head_sums(x, ones_bd):
    return jnp.concatenate([_dot(_bf(x[:, j:j + LANES]), ones_bd) for j in range(0, x.shape[1], LANES)], axis=1)


def _sigmoid(x):
    return 1.0 / (1.0 + jnp.exp(-x))


def _silu(x):
    return x * _sigmoid(x)


def _iota(shape, axis):
    return lax.broadcasted_iota(jnp.int32, shape, axis)


def _div_pow2(x, n):
    return lax.shift_right_logical(x, int(math.log2(n)))


def _mod_pow2(x, n):
    return lax.bitwise_and(x, n - 1)


def _layer_norm(h, w, b):
    mu = jnp.mean(h, axis=-1, keepdims=True)
    xc = h - mu
    var = jnp.mean(xc * xc, axis=-1, keepdims=True)
    return xc * lax.rsqrt(var + LN_EPS) * w + b


def _params(*sem):
    return pltpu.CompilerParams(dimension_semantics=sem, vmem_limit_bytes=VMEM_LIMIT)


def _const_spec(shape):
    nd = len(shape)
    return pl.BlockSpec(shape, lambda *_: (0,) * nd, pipeline_mode=pl.Buffered(1))


def _mod_kernel(c_ref, w_ref, b_ref, o_ref):
    o_ref[...] = _dot3(_silu(c_ref[...]), w_ref[...]) + b_ref[...]


def _modulation(cvec, w_mod, b_mod):
    rows, d = cvec.shape
    n = w_mod.shape[1]
    tn = 1024
    return pl.pallas_call(
        _mod_kernel,
        grid=(n // tn,),
        in_specs=[pl.BlockSpec((rows, d), lambda j: (0, 0)),
                  pl.BlockSpec((d, tn), lambda j: (0, j)),
                  pl.BlockSpec((1, tn), lambda j: (0, j))],
        out_specs=pl.BlockSpec((rows, tn), lambda j: (0, j)),
        out_shape=jax.ShapeDtypeStruct((rows, n), F32),
        compiler_params=_params("arbitrary"),
        name="modulation",
    )(cvec, w_mod, b_mod.reshape(1, n))


def _inproj_kernel(x_ref, mod_ref, w_ref, og_ref, on_ref, or_ref):
    nb, tm, _ = x_ref.shape
    xm = jnp.concatenate([_bf(x_ref[i] * (1.0 + mod_ref[i, 0, 1:2, :]) + mod_ref[i, 0, 0:1, :])
                          for i in range(nb)], axis=0)
    c0, c1 = GLA_COLS_P, GLA_COLS_P + NAT_COLS_P
    for o_ref, cols in ((og_ref, slice(0, c0)), (on_ref, slice(c0, c1)), (or_ref, slice(c1, None))):
        p = _dot(xm, w_ref[:, cols])
        for i in range(nb):
            o_ref[i] = p[i * tm:(i + 1) * tm]


def _inproj(xz, modtab, w_in_p, tm, nct):
    b, t, d = xz.shape
    nb = DENSE_BATCH
    ncols = w_in_p.shape[1]
    seg = lambda bi, j: (bi, jnp.where(j >= nct, 1, 0), 0, 0)
    row = lambda bi, j: (bi, j, 0)
    return pl.pallas_call(
        _inproj_kernel,
        grid=(b // nb, t // tm),
        in_specs=[pl.BlockSpec((nb, tm, d), row),
                  pl.BlockSpec((nb, 1, 6, d), seg),
                  _const_spec((d, ncols))],
        out_specs=[pl.BlockSpec((nb, tm, GLA_COLS_P), row),
                   pl.BlockSpec((nb, tm, NAT_COLS_P), row),
                   pl.BlockSpec((nb, tm, RW_COLS_P), row)],
        out_shape=[jax.ShapeDtypeStruct((b, t, GLA_COLS_P), F32),
                   jax.ShapeDtypeStruct((b, t, NAT_COLS_P), F32),
                   jax.ShapeDtypeStruct((b, t, RW_COLS_P), F32)],
        compiler_params=_params("arbitrary", "arbitrary"),
        name="inproj",
    )(xz, modtab, w_in_p)


def _tri_ones(direction):
    r = _iota((CHUNK, CHUNK), 0)
    c = _iota((CHUNK, CHUNK), 1)
    keep = (c <= r) if direction == 0 else (c >= r)
    return jnp.where(keep, 1.0, 0.0).astype(BF16)


def _block_ones(width, block):
    r = _div_pow2(_iota((width, width), 0), block)
    c = _div_pow2(_iota((width, width), 1), block)
    return jnp.where(r == c, 1.0, 0.0).astype(BF16)


def _stages(streams):
    streams = list(streams)
    while streams:
        alive = []
        for s in streams:
            try:
                next(s)
                alive.append(s)
            except StopIteration:
                pass
        streams = alive
        yield


def _lockstep(streams):
    for _ in _stages(streams):
        pass


def _run_two_way(chunk_fn, c_lo, n, n_batch):
    half = n // 2

    def first(i, carry):
        _lockstep([s for bi in range(n_batch) for s in (chunk_fn(bi, c_lo + i, 0, True, c_lo, n),
                                                        chunk_fn(bi, c_lo + n - 1 - i, 1, True, c_lo, n))])
        return carry

    def second(i, carry):
        _lockstep([s for bi in range(n_batch) for s in (chunk_fn(bi, c_lo + half + i, 0, False, c_lo, n),
                                                        chunk_fn(bi, c_lo + half - 1 - i, 1, False, c_lo, n))])
        return carry

    lax.fori_loop(0, half, first, 0)
    lax.fori_loop(0, half, second, 0)


def _gla_kernel(pg_ref, cos_ref, sin_ref, wg_ref, gb_ref, nw_ref, o_ref, s_ref, *, n_ctx, n_all):
    C = CHUNK
    s_ref[...] = jnp.zeros(s_ref.shape, F32)

    lane_k = _iota((1, GLA_QK_PAD), 1)
    lane_v = _iota((1, GLA_V), 1)
    head_k = [_div_pow2(lane_k, GLA_DK) == h for h in range(GLA_HEADS)]
    head_v = [_div_pow2(lane_v, GLA_DV) == h for h in range(GLA_HEADS)]
    even_lane = _mod_pow2(_iota((1, LANES), 1), 2) == 0
    rows = _iota((C, GLA_HEADS * C), 0)
    cols = _mod_pow2(_iota((C, GLA_HEADS * C), 1), C)
    causal = [cols <= rows, cols >= rows]
    tri = [_tri_ones(0), _tri_ones(1)]
    bd_v = _block_ones(LANES, GLA_DV)
    s_rows = _div_pow2(_iota((GLA_V, GLA_QK_PAD), 0), GLA_DV)
    s_cols = _div_pow2(_iota((GLA_V, GLA_QK_PAD), 1), GLA_DK)
    s_diag = s_rows == s_cols
    scale = GLA_DK ** -0.5

    def chunk(bi, c, direction, is_first, c_lo, n):
        del c_lo, n
        r0 = pl.multiple_of(c * C, C)
        rs = pl.ds(r0, C)
        cs = cos_ref[rs, :]
        sn = sin_ref[rs, :]

        def rope(col0):
            parts = []
            for j in range(GLA_QK_PAD // LANES):
                xb = pg_ref[bi, rs, col0 + j * LANES:col0 + (j + 1) * LANES]
                sw = jnp.where(even_lane, pltpu.roll(xb, LANES - 1, 1), pltpu.roll(xb, 1, 1))
                parts.append(xb * cs + sw * sn)
            return jnp.concatenate(parts, axis=1)

        q = rope(0)
        k = rope(GLA_QK_PAD)
        v = pg_ref[bi, rs, 2 * GLA_QK_PAD:2 * GLA_QK_PAD + GLA_V]
        dn = pg_ref[bi, rs, 2 * GLA_QK_PAD + 2 * GLA_V:]
        wcols = slice(direction * GLA_QK_PAD, (direction + 1) * GLA_QK_PAD)
        z = _dot(_bf(dn), wg_ref[:, wcols]) + gb_ref[:, wcols]
        yield
        logd = jax.nn.log_sigmoid(z) * (1.0 / GLA_TAU)
        bcum = _dot_exact_lhs(tri[direction], logd)
        yield
        b_end = bcum[C - 1:C, :] if direction == 0 else bcum[0:1, :]
        q_e = _bf(q * jnp.exp(bcum) * scale)
        k_e = k * jnp.exp(-bcum)
        k_l = _bf(k * jnp.exp(b_end - bcum))

        k_stack = jnp.concatenate([_bf(jnp.where(head_k[h], k_e, 0.0)) for h in range(GLA_HEADS)], axis=0)
        att = _dot(q_e, k_stack, NT)
        state = s_ref[bi, direction]
        o_inter = _dot(q_e, _bf(state), NT)
        kv = _dot(_bf(v), k_l, TN)
        yield
        att = _bf(jnp.where(causal[direction], att, 0.0))
        v_bd = jnp.concatenate([_bf(jnp.where(head_v[h], v, 0.0)) for h in range(GLA_HEADS)], axis=0)
        o = _dot(att, v_bd) + o_inter
        s_ref[bi, direction] = state * jnp.exp(b_end) + jnp.where(s_diag, kv, 0.0)
        yield

        if is_first:
            o_ref[bi, rs, :] = o
        else:
            tot = o_ref[bi, rs, :] + o
            g = pg_ref[bi, rs, 2 * GLA_QK_PAD + GLA_V:2 * GLA_QK_PAD + 2 * GLA_V]
            ms = _head_sums(tot * tot, bd_v) * (1.0 / GLA_DV)
            yield
            o_ref[bi, rs, :] = tot * lax.rsqrt(ms + LN_EPS) * nw_ref[...] * _silu(g)

    n_batch = pg_ref.shape[0]
    _run_two_way(chunk, 0, n_ctx, n_batch)
    _run_two_way(chunk, n_ctx, n_all - n_ctx, n_batch)


def _gla(pg, cos_t, sin_t, wg, gb, nw, lc):
    b, t, _ = pg.shape
    nb = MIXER_BATCH
    kern = functools.partial(_gla_kernel, n_ctx=lc // CHUNK, n_all=t // CHUNK)
    return pl.pallas_call(
        kern,
        grid=(b // nb,),
        in_specs=[pl.BlockSpec((nb, t, GLA_COLS_P), lambda i: (i, 0, 0), pipeline_mode=pl.Buffered(1)),
                  _const_spec((t, LANES)), _const_spec((t, LANES)),
                  _const_spec(wg.shape), _const_spec(gb.shape), _const_spec(nw.shape)],
        out_specs=pl.BlockSpec((nb, t, GLA_V), lambda i: (i, 0, 0)),
        out_shape=jax.ShapeDtypeStruct((b, t, GLA_V), F32),
        scratch_shapes=[pltpu.VMEM((nb, 2, GLA_V, GLA_QK_PAD), F32)],
        compiler_params=_params("arbitrary"),
        name="gla",
    )(pg, cos_t, sin_t, wg, gb, nw)


def _nat_kernel(pn_ref, bias_ref, o_ref, *, lc, n_rows):
    W = GRID_W
    KR = NAT_WIN_R
    scale = NAT_DH ** -0.5
    half = [_iota((1, LANES), 1) < NAT_DH, _iota((1, LANES), 1) >= NAT_DH]
    n_pairs = NAT_W // LANES

    def col(group, j):
        return slice(group * NAT_W + j * LANES, group * NAT_W + (j + 1) * LANES)

    def pair_stream(j, q_rows, key_rows, bias_of_head):
        nq = q_rows.size
        q = pn_ref[0, q_rows, col(0, j)] * scale
        qm = _bf(jnp.concatenate([jnp.where(half[0], q, 0.0), jnp.where(half[1], q, 0.0)], axis=0))
        keys = [_bf(pn_ref[0, 0:lc, col(1, j)])]
        vals = [_bf(pn_ref[0, 0:lc, col(2, j)])]
        if key_rows is not None:
            keys.insert(0, _bf(pn_ref[0, key_rows, col(1, j)]))
            vals.insert(0, _bf(pn_ref[0, key_rows, col(2, j)]))
        scores = [_dot(qm, kk, NT) for kk in keys]
        yield
        if key_rows is not None:
            scores[0] = scores[0] + jnp.concatenate([bias_of_head(2 * j), bias_of_head(2 * j + 1)], axis=0)
        m = scores[0].max(axis=-1, keepdims=True)
        for s in scores[1:]:
            m = jnp.maximum(m, s.max(axis=-1, keepdims=True))
        ps = [jnp.exp(s - m) for s in scores]
        l = ps[0].sum(axis=-1, keepdims=True)
        for p in ps[1:]:
            l = l + p.sum(axis=-1, keepdims=True)
        o = _dot(_bf(ps[0]), vals[0])
        for p, vv in zip(ps[1:], vals[1:]):
            o = o + _dot(_bf(p), vv)
        yield
        o = o / l
        o_ref[0, q_rows, col(0, j)] = jnp.where(half[0], o[0:nq], o[nq:])

    _lockstep([pair_stream(j, pl.ds(0, lc), None, None) for j in range(n_pairs)])

    def row_streams(r):
        start = jnp.clip(r - KR // 2, 0, n_rows - KR)
        delta = r - start
        q0 = pl.multiple_of(lc + r * W, W)
        k0 = pl.multiple_of(lc + start * W, W)
        return [pair_stream(j, pl.ds(q0, W), pl.ds(k0, KR * W), lambda h: bias_ref[h, delta])
                for j in range(n_pairs)]

    def row_body(i, carry):
        _lockstep([s for k in range(NAT_ROWS_PER_STEP) for s in row_streams(NAT_ROWS_PER_STEP * i + k)])
        return carry

    lax.fori_loop(0, n_rows // NAT_ROWS_PER_STEP, row_body, 0)


def _nat(pn, bias_tab, lc):
    b, t, _ = pn.shape
    kern = functools.partial(_nat_kernel, lc=lc, n_rows=(t - lc) // GRID_W)
    return pl.pallas_call(
        kern,
        grid=(b,),
        in_specs=[pl.BlockSpec((1, t, NAT_COLS_P), lambda i: (i, 0, 0)),
                  _const_spec(bias_tab.shape)],
        out_specs=pl.BlockSpec((1, t, NAT_W), lambda i: (i, 0, 0)),
        out_shape=jax.ShapeDtypeStruct((b, t, NAT_W), F32),
        compiler_params=_params("arbitrary"),
        name="nat",
    )(pn, bias_tab)


def _rw_kernel(pr_ref, mu_ref, w0_ref, wd_ref, a0_ref, wa_ref, wg_ref, kk_ref, ka_ref, rk_ref,
               gnw_ref, gnb_ref, o_ref, s_ref, *, n_ctx, n_all):
    C = CHUNK
    C2 = 2 * C
    n_pairs = RW_W // LANES
    t_all = n_all * C
    s_ref[...] = jnp.zeros(s_ref.shape, F32)

    lane = _iota((1, LANES), 1)
    half = [lane < RW_DH, lane >= RW_DH]
    row_id = _iota((C, 1), 0)
    tri = [_tri_ones(0), _tri_ones(1)]
    bd = _block_ones(LANES, RW_DH)
    r_c = _iota((C, C2), 0)
    c_c = _mod_pow2(_iota((C, C2), 1), C)
    incl = [c_c <= r_c, c_c >= r_c]
    r_s = _iota((C2, C2), 0)
    c_s = _iota((C2, C2), 1)
    strict = [c_s < r_s, c_s > r_s]
    n_double = int(math.log2(C))
    top = _iota((C2, LANES), 0) < C
    st_half = top == (_iota((C2, LANES), 1) < RW_DH)

    def seg_sum(x):
        return _head_sums(x, bd)

    def lora(x, w_ref, direction):
        return _dot(_bf(x), w_ref[direction])

    def chunk(bi, c, direction, is_first, c_lo, n):
        r0 = pl.multiple_of(c * C, C)
        rs = pl.ds(r0, C)
        p = pr_ref[bi, rs, :]
        prev_row = pr_ref[bi, pl.ds(pl.multiple_of(jnp.maximum(r0 - 8, 0), 8), 8), :][7:8, :]
        next_row = pr_ref[bi, pl.ds(pl.multiple_of(jnp.minimum(r0 + C, t_all - 8), 8), 8), :][0:1, :]
        prev_row = jnp.where(c == c_lo, 0.0, prev_row)
        next_row = jnp.where(c == c_lo + n - 1, 0.0, next_row)
        up = jnp.where(row_id == 0, prev_row, pltpu.roll(p, 1, 0))
        dn = jnp.where(row_id == C - 1, next_row, pltpu.roll(p, C - 1, 0))
        y = p + (0.5 * (up + dn) - p) * mu_ref[...]

        r = y[:, 0:RW_W]
        k = y[:, RW_W:2 * RW_W]
        v = y[:, 2 * RW_W:3 * RW_W]
        dd = y[:, 3 * RW_W:3 * RW_W + LANES]
        ad = y[:, 3 * RW_W + LANES:3 * RW_W + 2 * LANES]

        other = 1 - direction
        d_lin = lora(jnp.tanh(dd), wd_ref, direction)
        a_lin = lora(ad, wa_ref, direction)
        kk = k * kk_ref[...]
        kk_ss = seg_sum(kk * kk)
        if not is_first:
            a_lin_o = lora(ad, wa_ref, other)
            gate = _dot(_bf(_sigmoid(y[:, 3 * RW_W + 2 * LANES:])), wg_ref[...])
        yield
        logw = -RW_DECAY_SCALE * _sigmoid(w0_ref[direction] + d_lin)
        asig = _sigmoid(a0_ref[direction] + a_lin)
        kk = kk * lax.rsqrt(kk_ss + 1e-12)
        kmod = k * (1.0 + (asig - 1.0) * ka_ref[...])
        gcum = _dot_exact_lhs(tri[direction], logw)
        if not is_first:
            kmod_o = k * (1.0 + (_sigmoid(a0_ref[other] + a_lin_o) - 1.0) * ka_ref[...])
            bonus_w = seg_sum(r * (kmod + kmod_o) * rk_ref[...])
        yield
        g_end = gcum[C - 1:C, :] if direction == 0 else gcum[0:1, :]
        e_pos = jnp.exp(gcum)
        e_neg = jnp.exp(-gcum)
        r_t = r * e_pos
        a_t = -kk * jnp.exp(gcum - logw)
        b_t = kk * asig * e_neg
        k_t = kmod * e_neg
        decay_end = jnp.exp(g_end)

        outs = [None] * n_pairs

        def pair(j):
            cl = slice(j * LANES, (j + 1) * LANES)

            def stack(x):
                xb = x[:, cl]
                return jnp.concatenate([jnp.where(half[0], xb, 0.0), jnp.where(half[1], xb, 0.0)], axis=0)

            a_st = stack(a_t)
            v_st = _bf(stack(v))
            rhs = _bf(jnp.concatenate([stack(b_t), stack(k_t)], axis=0))
            lhs = _bf(jnp.concatenate([r_t[:, cl], a_st], axis=0))
            sc = _dot(lhs, rhs, NT)
            yield
            a_rb = jnp.where(incl[direction], sc[0:C, 0:C2], 0.0)
            a_rk = jnp.where(incl[direction], sc[0:C, C2:], 0.0)
            n_bd = jnp.where(strict[direction], sc[C:, 0:C2], 0.0)
            a_ak = jnp.where(strict[direction], sc[C:, C2:], 0.0)
            y2 = _dot(_bf(a_ak), v_st)
            npow = _bf(n_bd)
            yield
            x = jnp.concatenate([a_st[0:C] + pltpu.roll(y2[0:C], RW_DH, 1),
                                 pltpu.roll(a_st[C:], RW_DH, 1) + y2[C:]], axis=0)
            for i in range(n_double):
                if i + 1 < n_double:
                    both = _dot(npow, jnp.concatenate([_bf(x), npow], axis=1))
                    yield
                    x = x + both[:, 0:LANES]
                    npow = _bf(both[:, LANES:])
                else:
                    upd = _dot(npow, _bf(x))
                    yield
                    x = x + upd
            x_sw = jnp.concatenate([pltpu.roll(x[0:C], RW_DH, 1), pltpu.roll(x[C:], RW_DH, 1)], axis=0)
            a_hat = jnp.where(st_half, jnp.where(top, x, x_sw), 0.0)
            u_hat = jnp.where(st_half, jnp.where(top, x_sw, x), 0.0)
            state = s_ref[bi, direction, j]
            m3 = _dot(_bf(jnp.concatenate([r_t[:, cl], a_hat], axis=0)), _bf(state), NT)
            yield
            u = m3[C:] + u_hat
            uv = jnp.concatenate([_bf(u), v_st], axis=0)
            o_j = _dot(_bf(jnp.concatenate([a_rb, a_rk], axis=1)), uv)
            s_new = _dot(uv, rhs, TN)
            yield
            outs[j] = m3[0:C] + o_j
            s_ref[bi, direction, j] = (state + s_new) * decay_end[:, cl]

        yield from _stages([pair(j) for j in range(n_pairs)])
        o = jnp.concatenate(outs, axis=1)

        if is_first:
            o_ref[bi, rs, :] = o
        else:
            tot = o_ref[bi, rs, :] + o
            inv_n = 1.0 / RW_DH
            mean = seg_sum(tot) * inv_n
            yield
            xc = tot - mean
            var = seg_sum(xc * xc) * inv_n
            yield
            yn = xc * lax.rsqrt(var + RW_GN_EPS) * gnw_ref[...] + gnb_ref[...]
            o_ref[bi, rs, :] = (yn + bonus_w * v) * gate

    n_batch = pr_ref.shape[0]
    _run_two_way(chunk, 0, n_ctx, n_batch)
    _run_two_way(chunk, n_ctx, n_all - n_ctx, n_batch)


def _rwkv(pr, lc, mu, w0, wd, a0, wa, wg, k_k, k_a, r_k, gn_w, gn_b):
    b, t, _ = pr.shape
    nb = MIXER_BATCH
    kern = functools.partial(_rw_kernel, n_ctx=lc // CHUNK, n_all=t // CHUNK)
    consts = (mu, w0, wd, a0, wa, wg, k_k, k_a, r_k, gn_w, gn_b)
    return pl.pallas_call(
        kern,
        grid=(b // nb,),
        in_specs=[pl.BlockSpec((nb, t, RW_COLS_P), lambda i: (i, 0, 0), pipeline_mode=pl.Buffered(1))]
        + [_const_spec(a.shape) for a in consts],
        out_specs=pl.BlockSpec((nb, t, RW_W), lambda i: (i, 0, 0)),
        out_shape=jax.ShapeDtypeStruct((b, t, RW_W), F32),
        scratch_shapes=[pltpu.VMEM((nb, 2, RW_W // LANES, LANES, LANES), F32)],
        compiler_params=_params("arbitrary"),
        name="rwkv7",
    )(pr, *consts)


def _post_kernel(x_ref, a_ref, n_ref, c_ref, mod_ref, wo_ref, w13_ref, w2_ref, l1w_ref, l1b_ref, l2w_ref, l2b_ref,
                 o_ref, *, alpha, hidden, n_split):
    nb, tm, _ = x_ref.shape
    stack = lambda ref: jnp.concatenate([_bf(ref[i]) for i in range(nb)], axis=0)
    y = (_dot(stack(a_ref), wo_ref[0:GLA_V, :])
         + _dot(stack(n_ref), wo_ref[GLA_V:GLA_V + NAT_W, :])
         + _dot(stack(c_ref), wo_ref[GLA_V + NAT_W:, :]))
    x1 = [_layer_norm(alpha * x_ref[i] + mod_ref[i, 0, 2:3, :] * y[i * tm:(i + 1) * tm], l1w_ref[...], l1b_ref[...])
          for i in range(nb)]
    h = jnp.concatenate([_bf(x1[i] * (1.0 + mod_ref[i, 0, 4:5, :]) + mod_ref[i, 0, 3:4, :]) for i in range(nb)],
                        axis=0)
    hs = hidden // n_split
    f = None
    for k in range(n_split):
        gte = _dot(h, w13_ref[:, k * hs:(k + 1) * hs])
        up = _dot(h, w13_ref[:, hidden + k * hs:hidden + (k + 1) * hs])
        part = _dot(_bf(_silu(gte) * up), w2_ref[k * hs:(k + 1) * hs, :])
        f = part if f is None else f + part
    for i in range(nb):
        o_ref[i] = _layer_norm(alpha * x1[i] + mod_ref[i, 0, 5:6, :] * f[i * tm:(i + 1) * tm],
                               l2w_ref[...], l2b_ref[...])


def _post(xz, ya, yn, yc, modtab, w_out, w13, w2, ln1, ln2, tm, nct, row_off, alpha):
    b, t, d = xz.shape
    nb = DENSE_BATCH
    t_out = t - row_off * tm
    row = lambda bi, j: (bi, j + row_off, 0)
    seg = lambda bi, j: (bi, jnp.where(j + row_off >= nct, 1, 0), 0, 0)
    consts = (w_out, w13, w2, *ln1, *ln2)
    return pl.pallas_call(
        functools.partial(_post_kernel, alpha=alpha, hidden=w2.shape[0], n_split=2),
        grid=(b // nb, t_out // tm),
        in_specs=[pl.BlockSpec((nb, tm, d), row),
                  pl.BlockSpec((nb, tm, GLA_V), row),
                  pl.BlockSpec((nb, tm, NAT_W), row),
                  pl.BlockSpec((nb, tm, RW_W), row),
                  pl.BlockSpec((nb, 1, 6, d), seg)] + [_const_spec(a.shape) for a in consts],
        out_specs=pl.BlockSpec((nb, tm, d), lambda bi, j: (bi, j, 0)),
        out_shape=jax.ShapeDtypeStruct((b, t_out, d), F32),
        compiler_params=_params("arbitrary", "arbitrary"),
        name="outproj_swiglu",
    )(xz, ya, yn, yc, modtab, *consts)


def _rope_tables(lc, l):
    t = jnp.arange(l, dtype=jnp.int32)
    row = (t // GRID_W).astype(F32)
    col = (t % GRID_W).astype(F32)
    n_freq = GLA_DK // 4
    inv = ROPE_BASE ** (-jnp.arange(n_freq, dtype=F32) / n_freq)
    ang = jnp.concatenate([row[:, None] * inv, col[:, None] * inv], axis=-1)
    pair = (np.arange(LANES) % GLA_DK) // 2
    sign = np.where(np.arange(LANES) % 2 == 0, -1.0, 1.0).astype(np.float32)
    cos_l = jnp.cos(ang)[:, pair]
    sin_l = jnp.sin(ang)[:, pair] * sign
    cos_t = jnp.concatenate([jnp.ones((lc, LANES), F32), cos_l], axis=0)
    sin_t = jnp.concatenate([jnp.zeros((lc, LANES), F32), sin_l], axis=0)
    return cos_t, sin_t


def _nat_bias_table(rpb):
    KR, KC, W = NAT_WIN_R, NAT_WIN_C, GRID_W
    cq = np.arange(W)
    col0 = np.clip(cq - KC // 2, 0, W - KC)
    in_win = (cq[None, :] >= col0[:, None]) & (cq[None, :] < col0[:, None] + KC)
    dc = np.clip(cq[None, :] - cq[:, None], -(KC - 1), KC - 1) + KC - 1
    delta = np.arange(KR)
    dr = np.arange(KR)[None, :] - delta[:, None] + NAT_WIN_R - 1
    tab = rpb[:, dr][:, :, :, dc]
    tab = jnp.where(in_win[None, None, None], tab, -jnp.inf)
    tab = tab.transpose(0, 1, 3, 2, 4)
    return tab.reshape(NAT_HEADS, KR, W, KR * W)


def _pad_cols(w, width):
    return jnp.pad(w, ((0, 0), (0, width - w.shape[1])))


def _layer_params(i, w_in, gla_gate_up, gla_gate_b, gla_norm_w, nat_rpb, rw_mu, rw_w0, rw_wd2, rw_a0, rw_wa2,
                  rw_wg2, rw_k_k, rw_k_a, rw_r_k, rw_gn_w, rw_gn_b, w_out, ln1_w, ln1_b, ffn_w13, ffn_w2,
                  ln2_w, ln2_b):
    wi = w_in[i]
    o = 0
    pieces = []
    for size, width in ((GLA_QK, GLA_QK_PAD), (GLA_QK, GLA_QK_PAD), (GLA_V, GLA_V), (GLA_V, GLA_V),
                        (2 * GLA_GATE_RANK, LANES)):
        pieces.append(_pad_cols(wi[:, o:o + size], width))
        o += size
    pieces.append(wi[:, o:])
    p = {"w_in": _bf(jnp.concatenate(pieces, axis=1))}

    wg = jnp.zeros((LANES, 2 * GLA_QK_PAD), F32)
    wg = wg.at[0:GLA_GATE_RANK, 0:GLA_QK].set(gla_gate_up[i, 0])
    wg = wg.at[GLA_GATE_RANK:2 * GLA_GATE_RANK, GLA_QK_PAD:GLA_QK_PAD + GLA_QK].set(gla_gate_up[i, 1])
    p["gla_wg"] = _bf(wg)
    p["gla_gb"] = jnp.concatenate([_pad_cols(gla_gate_b[i, 0:1], GLA_QK_PAD),
                                   _pad_cols(gla_gate_b[i, 1:2], GLA_QK_PAD)], axis=1)
    p["gla_nw"] = jnp.tile(gla_norm_w[i], GLA_HEADS).reshape(1, GLA_V)
    p["nat_bias"] = _nat_bias_table(nat_rpb[i])

    def lora_pair(w, rank):
        z = jnp.zeros((2, LANES, RW_W), F32)
        z = z.at[0, 0:rank].set(w[0])
        return z.at[1, rank:2 * rank].set(w[1])

    row = lambda a: a.reshape(1, -1)
    p["rw"] = (row(rw_mu[i]), rw_w0[i].reshape(2, 1, RW_W), _bf(lora_pair(rw_wd2[i], RW_DECAY_RANK)),
               rw_a0[i].reshape(2, 1, RW_W), _bf(lora_pair(rw_wa2[i], RW_A_RANK)), _bf(rw_wg2[i]),
               row(rw_k_k[i]), row(rw_k_a[i]), row(rw_r_k[i]), row(rw_gn_w[i]), row(rw_gn_b[i]))
    p["w_out"] = _bf(w_out[i])
    p["ln1"] = (row(ln1_w[i]), row(ln1_b[i]))
    p["w13"] = _bf(ffn_w13[i])
    p["w2"] = _bf(ffn_w2[i])
    p["ln2"] = (row(ln2_w[i]), row(ln2_b[i]))
    return p


def kernel(x, c, ctx, c_ctx, w_mod, b_mod, w_in, gla_gate_up, gla_gate_b, gla_norm_w, nat_rpb, rw_mu, rw_w0,
           rw_wd2, rw_a0, rw_wa2, rw_wg2, rw_k_k, rw_k_a, rw_r_k, rw_gn_w, rw_gn_b, w_out, ln1_w, ln1_b,
           ffn_w13, ffn_w2, ln2_w, ln2_b):
    b, l, d = x.shape
    lc = ctx.shape[1]
    depth = w_mod.shape[0]
    alpha = (2.0 * depth) ** 0.25
    tm = 256 if lc % 256 == 0 else 128
    assert lc % tm == 0 and l % tm == 0 and lc % (2 * CHUNK) == 0 and l % (2 * CHUNK) == 0
    assert l % (NAT_ROWS_PER_STEP * GRID_W) == 0 and l // GRID_W >= NAT_WIN_R
    assert b % DENSE_BATCH == 0 and b % MIXER_BATCH == 0
    nct = lc // tm

    cos_t, sin_t = _rope_tables(lc, l)
    rows = -(-(b + 1) // 8) * 8
    cvec = jnp.concatenate([c, c_ctx[None], jnp.zeros((rows - b - 1, d), F32)], axis=0)
    xz = jnp.concatenate([ctx, x], axis=1)

    for i in range(depth):
        last = i == depth - 1
        p = _layer_params(i, w_in, gla_gate_up, gla_gate_b, gla_norm_w, nat_rpb, rw_mu, rw_w0, rw_wd2, rw_a0,
                          rw_wa2, rw_wg2, rw_k_k, rw_k_a, rw_r_k, rw_gn_w, rw_gn_b, w_out, ln1_w, ln1_b,
                          ffn_w13, ffn_w2, ln2_w, ln2_b)
        m = _modulation(cvec, w_mod[i], b_mod[i])
        ml = m[:b].reshape(b, 1, 6, d)
        mc = jnp.broadcast_to(m[b].reshape(1, 1, 6, d), (b, 1, 6, d))
        modtab = jnp.concatenate([mc, ml], axis=1)

        pg, pn, pr = _inproj(xz, modtab, p["w_in"], tm, nct)
        ya = _gla(pg, cos_t, sin_t, p["gla_wg"], p["gla_gb"], p["gla_nw"], lc)
        yn = _nat(pn, p["nat_bias"], lc)
        yc = _rwkv(pr, lc, *p["rw"])
        row_off = nct if last else 0
        xz = _post(xz, ya, yn, yc, modtab, p["w_out"], p["w13"], p["w2"], p["ln1"], p["ln2"], tm, nct, row_off,
                   alpha)
    return xz
```

```python
import functools
import math

import jax
import jax.numpy as jnp
import numpy as np
from jax import lax
from jax.experimental import pallas as pl
from jax.experimental.pallas import tpu as pltpu

F32 = jnp.float32
BF16 = jnp.bfloat16

GRID_W = 64
GLA_HEADS, GLA_DK, GLA_DV, GLA_GATE_RANK, GLA_TAU = 6, 32, 64, 16, 16.0
NAT_HEADS, NAT_DH, NAT_WIN_R, NAT_WIN_C = 4, 64, 8, 16
RW_HEADS, RW_DH, RW_DECAY_RANK, RW_A_RANK, RW_GATE_RANK = 6, 64, 64, 64, 128
RW_GN_EPS = 64e-5
RW_DECAY_SCALE = math.exp(-0.5)
ROPE_BASE = 10000.0
LN_EPS = 1e-5

GLA_QK = GLA_HEADS * GLA_DK
GLA_V = GLA_HEADS * GLA_DV
NAT_W = NAT_HEADS * NAT_DH
RW_W = RW_HEADS * RW_DH

LANES = 128
CHUNK = 64
GLA_QK_PAD = 256
GLA_COLS_P = 2 * GLA_QK_PAD + 2 * GLA_V + LANES
NAT_COLS_P = 3 * NAT_W
RW_COLS_P = 3 * RW_W + 3 * LANES
VMEM_LIMIT = 56 * 1024 * 1024
DENSE_BATCH = 2
MIXER_BATCH = 2
GLA_BATCH = 4
NAT_ROWS_PER_STEP = 4

NN = (((1,), (0,)), ((), ()))
NT = (((1,), (1,)), ((), ()))
TN = (((0,), (0,)), ((), ()))


def _dot(a, b, dims=NN):
    return lax.dot_general(a, b, dims, preferred_element_type=F32)


def _bf(x):
    return x.astype(BF16)


def _split(x):
    hi = x.astype(BF16)
    lo = (x - hi.astype(F32)).astype(BF16)
    return hi, lo


def _dot3(a, b, dims=NN):
    ah, al = _split(a)
    bh, bl = _split(b)
    return _dot(ah, bh, dims) + (_dot(ah, bl, dims) + _dot(al, bh, dims))


def _dot_exact_lhs(a_bf, b, dims=NN):
    bh, bl = _split(b)
    return _dot(a_bf, bh, dims) + _dot(a_bf, bl, dims)


def _head_sums(x, ones_bd):
    return jnp.concatenate([_dot(_bf(x[:, j:j + LANES]), ones_bd) for j in range(0, x.shape[1], LANES)], axis=1)


def _sigmoid(x):
    return 1.0 / (1.0 + jnp.exp(-x))


def _silu(x):
    return x * _sigmoid(x)


def _iota(shape, axis):
    return lax.broadcasted_iota(jnp.int32, shape, axis)


def _div_pow2(x, n):
    return lax.shift_right_logical(x, int(math.log2(n)))


def _mod_pow2(x, n):
    return lax.bitwise_and(x, n - 1)


def _layer_norm(h, w, b):
    mu = jnp.mean(h, axis=-1, keepdims=True)
    xc = h - mu
    var = jnp.mean(xc * xc, axis=-1, keepdims=True)
    return xc * lax.rsqrt(var + LN_EPS) * w + b


def _params(*sem):
    return pltpu.CompilerParams(dimension_semantics=sem, vmem_limit_bytes=VMEM_LIMIT)


def _const_spec(shape):
    nd = len(shape)
    return pl.BlockSpec(shape, lambda *_: (0,) * nd, pipeline_mode=pl.Buffered(1))


def _mod_kernel(c_ref, w_ref, b_ref, o_ref):
    o_ref[...] = _dot3(_silu(c_ref[...]), w_ref[...]) + b_ref[...]


def _modulation(cvec, w_mod, b_mod):
    rows, d = cvec.shape
    n = w_mod.shape[1]
    tn = 1024
    return pl.pallas_call(
        _mod_kernel,
        grid=(n // tn,),
        in_specs=[pl.BlockSpec((rows, d), lambda j: (0, 0)),
                  pl.BlockSpec((d, tn), lambda j: (0, j)),
                  pl.BlockSpec((1, tn), lambda j: (0, j))],
        out_specs=pl.BlockSpec((rows, tn), lambda j: (0, j)),
        out_shape=jax.ShapeDtypeStruct((rows, n), F32),
        compiler_params=_params("arbitrary"),
        name="modulation",
    )(cvec, w_mod, b_mod.reshape(1, n))


def _inproj_kernel(x_ref, mod_ref, w_ref, cos_ref, sin_ref, og_ref, on_ref, or_ref):
    nb, tm, _ = x_ref.shape
    xm = jnp.concatenate([_bf(x_ref[i] * (1.0 + mod_ref[i, 0, 1:2, :]) + mod_ref[i, 0, 0:1, :])
                          for i in range(nb)], axis=0)
    c0, c1 = GLA_COLS_P, GLA_COLS_P + NAT_COLS_P
    for o_ref, cols in ((on_ref, slice(c0, c1)), (or_ref, slice(c1, None)), (og_ref, slice(2 * GLA_QK_PAD, c0))):
        p = _bf(_dot(xm, w_ref[:, cols]))
        col0 = 2 * GLA_QK_PAD if o_ref is og_ref else 0
        for i in range(nb):
            o_ref[i, :, col0:] = p[i * tm:(i + 1) * tm]
    even_lane = _mod_pow2(_iota((1, LANES), 1), 2) == 0
    cs = jnp.concatenate([cos_ref[...]] * nb, axis=0)
    sn = jnp.concatenate([sin_ref[...]] * nb, axis=0)
    qk = _dot(xm, w_ref[:, 0:2 * GLA_QK_PAD])
    for j in range(2 * GLA_QK_PAD // LANES):
        xb = qk[:, j * LANES:(j + 1) * LANES]
        sw = jnp.where(even_lane, pltpu.roll(xb, LANES - 1, 1), pltpu.roll(xb, 1, 1))
        rot = _bf(xb * cs + sw * sn)
        for i in range(nb):
            og_ref[i, :, j * LANES:(j + 1) * LANES] = rot[i * tm:(i + 1) * tm]


def _inproj(xz, modtab, w_in_p, cos_t, sin_t, tm, nct):
    b, t, d = xz.shape
    nb = DENSE_BATCH
    ncols = w_in_p.shape[1]
    seg = lambda bi, j: (bi, jnp.where(j >= nct, 1, 0), 0, 0)
    row = lambda bi, j: (bi, j, 0)
    return pl.pallas_call(
        _inproj_kernel,
        grid=(b // nb, t // tm),
        in_specs=[pl.BlockSpec((nb, tm, d), row),
                  pl.BlockSpec((nb, 1, 6, d), seg),
                  _const_spec((d, ncols)),
                  pl.BlockSpec((tm, LANES), lambda bi, j: (j, 0)),
                  pl.BlockSpec((tm, LANES), lambda bi, j: (j, 0))],
        out_specs=[pl.BlockSpec((nb, tm, GLA_COLS_P), row),
                   pl.BlockSpec((nb, tm, NAT_COLS_P), row),
                   pl.BlockSpec((nb, tm, RW_COLS_P), row)],
        out_shape=[jax.ShapeDtypeStruct((b, t, GLA_COLS_P), BF16),
                   jax.ShapeDtypeStruct((b, t, NAT_COLS_P), BF16),
                   jax.ShapeDtypeStruct((b, t, RW_COLS_P), BF16)],
        compiler_params=_params("arbitrary", "arbitrary"),
        name="inproj",
    )(xz, modtab, w_in_p, cos_t, sin_t)


def _tri_ones(direction):
    r = _iota((CHUNK, CHUNK), 0)
    c = _iota((CHUNK, CHUNK), 1)
    keep = (c <= r) if direction == 0 else (c >= r)
    return jnp.where(keep, 1.0, 0.0).astype(BF16)


def _block_ones(width, block):
    r = _div_pow2(_iota((width, width), 0), block)
    c = _div_pow2(_iota((width, width), 1), block)
    return jnp.where(r == c, 1.0, 0.0).astype(BF16)


def _stages(streams):
    streams = list(streams)
    while streams:
        alive = []
        for s in streams:
            try:
                next(s)
                alive.append(s)
            except StopIteration:
                pass
        streams = alive
        yield


def _lockstep(streams):
    for _ in _stages(streams):
        pass


def _run_two_way(chunk_fn, c_lo, n, n_batch):
    half = n // 2

    def first(i, carry):
        _lockstep([s for bi in range(n_batch) for s in (chunk_fn(bi, c_lo + i, 0, True, c_lo, n),
                                                        chunk_fn(bi, c_lo + n - 1 - i, 1, True, c_lo, n))])
        return carry

    def second(i, carry):
        _lockstep([s for bi in range(n_batch) for s in (chunk_fn(bi, c_lo + half + i, 0, False, c_lo, n),
                                                        chunk_fn(bi, c_lo + half - 1 - i, 1, False, c_lo, n))])
        return carry

    lax.fori_loop(0, half, first, 0)
    lax.fori_loop(0, half, second, 0)


def _gla_kernel(pg_ref, wg_ref, gb_ref, nw_ref, o_ref, s_ref, *, n_ctx, n_all):
    C = CHUNK
    s_ref[...] = jnp.zeros(s_ref.shape, F32)

    lane_k = _iota((1, GLA_QK_PAD), 1)
    lane_v = _iota((1, GLA_V), 1)
    head_k = [_div_pow2(lane_k, GLA_DK) == h for h in range(GLA_HEADS)]
    head_v = [_div_pow2(lane_v, GLA_DV) == h for h in range(GLA_HEADS)]
    rows = _iota((C, GLA_HEADS * C), 0)
    cols = _mod_pow2(_iota((C, GLA_HEADS * C), 1), C)
    causal = [cols <= rows, cols >= rows]
    tri = [_tri_ones(0), _tri_ones(1)]
    bd_v = _block_ones(LANES, GLA_DV)
    s_rows = _div_pow2(_iota((GLA_V, GLA_QK_PAD), 0), GLA_DV)
    s_cols = _div_pow2(_iota((GLA_V, GLA_QK_PAD), 1), GLA_DK)
    s_diag = s_rows == s_cols
    scale = GLA_DK ** -0.5

    def chunk(bi, c, direction, is_first, c_lo, n):
        del c_lo, n
        r0 = pl.multiple_of(c * C, C)
        rs = pl.ds(r0, C)
        q = pg_ref[bi, rs, 0:GLA_QK_PAD].astype(F32)
        k = pg_ref[bi, rs, GLA_QK_PAD:2 * GLA_QK_PAD].astype(F32)
        v = pg_ref[bi, rs, 2 * GLA_QK_PAD:2 * GLA_QK_PAD + GLA_V]
        dn = pg_ref[bi, rs, 2 * GLA_QK_PAD + 2 * GLA_V:]
        wcols = slice(direction * GLA_QK_PAD, (direction + 1) * GLA_QK_PAD)
        z = _dot(dn, wg_ref[:, wcols]) + gb_ref[:, wcols]
        yield
        logd = jax.nn.log_sigmoid(z) * (1.0 / GLA_TAU)
        bcum = _dot_exact_lhs(tri[direction], logd)
        yield
        b_end = bcum[C - 1:C, :] if direction == 0 else bcum[0:1, :]
        q_e = _bf(q * jnp.exp(bcum) * scale)
        k_e = _bf(k * jnp.exp(-bcum))
        k_l = _bf(k * jnp.exp(b_end - bcum))

        k_stack = jnp.concatenate([jnp.where(head_k[h], k_e, jnp.zeros_like(k_e)) for h in range(GLA_HEADS)],
                                  axis=0)
        att = _dot(q_e, k_stack, NT)
        state = s_ref[bi, direction]
        o_inter = _dot(q_e, _bf(state), NT)
        kv = _dot(v, k_l, TN)
        yield
        att = _bf(jnp.where(causal[direction], att, 0.0))
        v_bd = jnp.concatenate([jnp.where(head_v[h], v, jnp.zeros_like(v)) for h in range(GLA_HEADS)], axis=0)
        o = _dot(att, v_bd) + o_inter
        decay = jnp.exp(b_end)
        for h in range(GLA_HEADS):
            rh = slice(h * GLA_DV, (h + 1) * GLA_DV)
            ch = slice(h * GLA_DK // LANES * LANES, (h * GLA_DK // LANES + 1) * LANES)
            s_ref[bi, direction, rh, ch] = state[rh, ch] * decay[:, ch] + jnp.where(s_diag[rh, ch], kv[rh, ch], 0.0)
        yield

        if is_first:
            o_ref[bi, rs, :] = _bf(o)
        else:
            tot = o_ref[bi, rs, :].astype(F32) + o
            g = pg_ref[bi, rs, 2 * GLA_QK_PAD + GLA_V:2 * GLA_QK_PAD + 2 * GLA_V].astype(F32)
            ms = _head_sums(tot * tot, bd_v) * (1.0 / GLA_DV)
            yield
            o_ref[bi, rs, :] = _bf(tot * lax.rsqrt(ms + LN_EPS) * nw_ref[...] * _silu(g))

    n_batch = pg_ref.shape[0]
    _run_two_way(chunk, 0, n_ctx, n_batch)
    _run_two_way(chunk, n_ctx, n_all - n_ctx, n_batch)


def _gla(pg, wg, gb, nw, lc):
    b, t, _ = pg.shape
    nb = GLA_BATCH
    kern = functools.partial(_gla_kernel, n_ctx=lc // CHUNK, n_all=t // CHUNK)
    return pl.pallas_call(
        kern,
        grid=(b // nb,),
        in_specs=[pl.BlockSpec((nb, t, GLA_COLS_P), lambda i: (i, 0, 0), pipeline_mode=pl.Buffered(1)),
                  _const_spec(wg.shape), _const_spec(gb.shape), _const_spec(nw.shape)],
        out_specs=pl.BlockSpec((nb, t, GLA_V), lambda i: (i, 0, 0)),
        out_shape=jax.ShapeDtypeStruct((b, t, GLA_V), BF16),
        scratch_shapes=[pltpu.VMEM((nb, 2, GLA_V, GLA_QK_PAD), F32)],
        compiler_params=_params("arbitrary"),
        name="gla",
    )(pg, wg, gb, nw)


def _nat_kernel(pn_ref, bias_ref, o_ref, *, lc, n_rows):
    W = GRID_W
    KR = NAT_WIN_R
    scale = NAT_DH ** -0.5
    half = [_iota((1, LANES), 1) < NAT_DH, _iota((1, LANES), 1) >= NAT_DH]
    n_pairs = NAT_W // LANES

    def col(group, j):
        return slice(group * NAT_W + j * LANES, group * NAT_W + (j + 1) * LANES)

    def pair_stream(j, q_rows, key_rows, bias_of_head):
        nq = q_rows.size
        q = pn_ref[0, q_rows, col(0, j)] * scale
        qm = _bf(jnp.concatenate([jnp.where(half[0], q, 0.0), jnp.where(half[1], q, 0.0)], axis=0))
        keys = [_bf(pn_ref[0, 0:lc, col(1, j)])]
        vals = [_bf(pn_ref[0, 0:lc, col(2, j)])]
        if key_rows is not None:
            keys.insert(0, _bf(pn_ref[0, key_rows, col(1, j)]))
            vals.insert(0, _bf(pn_ref[0, key_rows, col(2, j)]))
        scores = [_dot(qm, kk, NT) for kk in keys]
        yield
        if key_rows is not None:
            scores[0] = scores[0] + jnp.concatenate([bias_of_head(2 * j), bias_of_head(2 * j + 1)], axis=0)
        m = scores[0].max(axis=-1, keepdims=True)
        for s in scores[1:]:
            m = jnp.maximum(m, s.max(axis=-1, keepdims=True))
        ps = [jnp.exp(s - m) for s in scores]
        l = ps[0].sum(axis=-1, keepdims=True)
        for p in ps[1:]:
            l = l + p.sum(axis=-1, keepdims=True)
        o = _dot(_bf(ps[0]), vals[0])
        for p, vv in zip(ps[1:], vals[1:]):
            o = o + _dot(_bf(p), vv)
        yield
        o = o / l
        o_ref[0, q_rows, col(0, j)] = _bf(jnp.where(half[0], o[0:nq], o[nq:]))

    _lockstep([pair_stream(j, pl.ds(0, lc), None, None) for j in range(n_pairs)])

    def row_streams(r):
        start = jnp.clip(r - KR // 2, 0, n_rows - KR)
        delta = r - start
        q0 = pl.multiple_of(lc + r * W, W)
        k0 = pl.multiple_of(lc + start * W, W)
        return [pair_stream(j, pl.ds(q0, W), pl.ds(k0, KR * W), lambda h: bias_ref[h, delta])
                for j in range(n_pairs)]

    def row_body(i, carry):
        _lockstep([s for k in range(NAT_ROWS_PER_STEP) for s in row_streams(NAT_ROWS_PER_STEP * i + k)])
        return carry

    lax.fori_loop(0, n_rows // NAT_ROWS_PER_STEP, row_body, 0)


def _nat(pn, bias_tab, lc):
    b, t, _ = pn.shape
    kern = functools.partial(_nat_kernel, lc=lc, n_rows=(t - lc) // GRID_W)
    return pl.pallas_call(
        kern,
        grid=(b,),
        in_specs=[pl.BlockSpec((1, t, NAT_COLS_P), lambda i: (i, 0, 0)),
                  _const_spec(bias_tab.shape)],
        out_specs=pl.BlockSpec((1, t, NAT_W), lambda i: (i, 0, 0)),
        out_shape=jax.ShapeDtypeStruct((b, t, NAT_W), BF16),
        compiler_params=_params("arbitrary"),
        name="nat",
    )(pn, bias_tab)


def _rw_kernel(pr_ref, mu_ref, w0_ref, wd_ref, a0_ref, wa_ref, wg_ref, kk_ref, ka_ref, rk_ref,
               gnw_ref, gnb_ref, o_ref, s_ref, *, n_ctx, n_all):
    C = CHUNK
    C2 = 2 * C
    n_pairs = RW_W // LANES
    t_all = n_all * C
    s_ref[...] = jnp.zeros(s_ref.shape, F32)

    lane = _iota((1, LANES), 1)
    half = [lane < RW_DH, lane >= RW_DH]
    row_id = _iota((C, 1), 0)
    tri = [_tri_ones(0), _tri_ones(1)]
    bd = _block_ones(LANES, RW_DH)
    r_c = _iota((C, C2), 0)
    c_c = _mod_pow2(_iota((C, C2), 1), C)
    incl = [c_c <= r_c, c_c >= r_c]
    r_s = _iota((C2, C2), 0)
    c_s = _iota((C2, C2), 1)
    strict = [c_s < r_s, c_s > r_s]
    n_double = int(math.log2(C))
    top = _iota((C2, LANES), 0) < C
    st_half = top == (_iota((C2, LANES), 1) < RW_DH)

    def seg_sum(x):
        return _head_sums(x, bd)

    def lora(x, w_ref, direction):
        return _dot(_bf(x), w_ref[direction])

    def chunk(bi, c, direction, is_first, c_lo, n):
        r0 = pl.multiple_of(c * C, C)
        rs = pl.ds(r0, C)
        p = pr_ref[bi, rs, :].astype(F32)
        halo = 16
        prev_row = pr_ref[bi, pl.ds(pl.multiple_of(jnp.maximum(r0 - halo, 0), halo), halo), :].astype(F32)[
            halo - 1:halo, :]
        next_row = pr_ref[bi, pl.ds(pl.multiple_of(jnp.minimum(r0 + C, t_all - halo), halo), halo), :].astype(F32)[
            0:1, :]
        prev_row = jnp.where(c == c_lo, 0.0, prev_row)
        next_row = jnp.where(c == c_lo + n - 1, 0.0, next_row)
        up = jnp.where(row_id == 0, prev_row, pltpu.roll(p, 1, 0))
        dn = jnp.where(row_id == C - 1, next_row, pltpu.roll(p, C - 1, 0))
        y = p + (0.5 * (up + dn) - p) * mu_ref[...]

        r = y[:, 0:RW_W]
        k = y[:, RW_W:2 * RW_W]
        v = y[:, 2 * RW_W:3 * RW_W]
        dd = y[:, 3 * RW_W:3 * RW_W + LANES]
        ad = y[:, 3 * RW_W + LANES:3 * RW_W + 2 * LANES]

        other = 1 - direction
        d_lin = lora(jnp.tanh(dd), wd_ref, direction)
        a_lin = lora(ad, wa_ref, direction)
        kk = k * kk_ref[...]
        kk_ss = seg_sum(kk * kk)
        if not is_first:
            a_lin_o = lora(ad, wa_ref, other)
            gate = _dot(_bf(_sigmoid(y[:, 3 * RW_W + 2 * LANES:])), wg_ref[...])
        yield
        logw = -RW_DECAY_SCALE * _sigmoid(w0_ref[direction] + d_lin)
        asig = _sigmoid(a0_ref[direction] + a_lin)
        kk = kk * lax.rsqrt(kk_ss + 1e-12)
        kmod = k * (1.0 + (asig - 1.0) * ka_ref[...])
        gcum = _dot_exact_lhs(tri[direction], logw)
        if not is_first:
            kmod_o = k * (1.0 + (_sigmoid(a0_ref[other] + a_lin_o) - 1.0) * ka_ref[...])
            bonus_w = seg_sum(r * (kmod + kmod_o) * rk_ref[...])
        yield
        g_end = gcum[C - 1:C, :] if direction == 0 else gcum[0:1, :]
        e_pos = jnp.exp(gcum)
        e_neg = jnp.exp(-gcum)
        r_t = r * e_pos
        a_t = -kk * jnp.exp(gcum - logw)
        b_t = kk * asig * e_neg
        k_t = kmod * e_neg
        decay_end = jnp.exp(g_end)

        outs = [None] * n_pairs

        def pair(j):
            cl = slice(j * LANES, (j + 1) * LANES)

            def stack(x):
                xb = x[:, cl]
                return jnp.concatenate([jnp.where(half[0], xb, 0.0), jnp.where(half[1], xb, 0.0)], axis=0)

            a_st = stack(a_t)
            v_st = _bf(stack(v))
            rhs = _bf(jnp.concatenate([stack(b_t), stack(k_t)], axis=0))
            lhs = _bf(jnp.concatenate([r_t[:, cl], a_st], axis=0))
            sc = _dot(lhs, rhs, NT)
            yield
            a_rb = jnp.where(incl[direction], sc[0:C, 0:C2], 0.0)
            a_rk = jnp.where(incl[direction], sc[0:C, C2:], 0.0)
            n_bd = jnp.where(strict[direction], sc[C:, 0:C2], 0.0)
            a_ak = jnp.where(strict[direction], sc[C:, C2:], 0.0)
            y2 = _dot(_bf(a_ak), v_st)
            npow = _bf(n_bd)
            yield
            x = jnp.concatenate([a_st[0:C] + pltpu.roll(y2[0:C], RW_DH, 1),
                                 pltpu.roll(a_st[C:], RW_DH, 1) + y2[C:]], axis=0)
            for i in range(n_double):
                if i + 1 < n_double:
                    both = _dot(npow, jnp.concatenate([_bf(x), npow], axis=1))
                    yield
                    x = x + both[:, 0:LANES]
                    npow = _bf(both[:, LANES:])
                else:
                    upd = _dot(npow, _bf(x))
                    yield
                    x = x + upd
            x_sw = jnp.concatenate([pltpu.roll(x[0:C], RW_DH, 1), pltpu.roll(x[C:], RW_DH, 1)], axis=0)
            a_hat = jnp.where(st_half, jnp.where(top, x, x_sw), 0.0)
            u_hat = jnp.where(st_half, jnp.where(top, x_sw, x), 0.0)
            state = s_ref[bi, direction, j]
            m3 = _dot(_bf(jnp.concatenate([r_t[:, cl], a_hat], axis=0)), _bf(state), NT)
            yield
            u = m3[C:] + u_hat
            uv = jnp.concatenate([_bf(u), v_st], axis=0)
            o_j = _dot(_bf(jnp.concatenate([a_rb, a_rk], axis=1)), uv)
            s_new = _dot(uv, rhs, TN)
            yield
            outs[j] = m3[0:C] + o_j
            s_ref[bi, direction, j] = (state + s_new) * decay_end[:, cl]

        yield from _stages([pair(j) for j in range(n_pairs)])
        o = jnp.concatenate(outs, axis=1)

        if is_first:
            o_ref[bi, rs, :] = _bf(o)
        else:
            tot = o_ref[bi, rs, :].astype(F32) + o
            inv_n = 1.0 / RW_DH
            mean = seg_sum(tot) * inv_n
            yield
            xc = tot - mean
            var = seg_sum(xc * xc) * inv_n
            yield
            yn = xc * lax.rsqrt(var + RW_GN_EPS) * gnw_ref[...] + gnb_ref[...]
            o_ref[bi, rs, :] = _bf((yn + bonus_w * v) * gate)

    n_batch = pr_ref.shape[0]
    _run_two_way(chunk, 0, n_ctx, n_batch)
    _run_two_way(chunk, n_ctx, n_all - n_ctx, n_batch)


def _rwkv(pr, lc, mu, w0, wd, a0, wa, wg, k_k, k_a, r_k, gn_w, gn_b):
    b, t, _ = pr.shape
    nb = MIXER_BATCH
    kern = functools.partial(_rw_kernel, n_ctx=lc // CHUNK, n_all=t // CHUNK)
    consts = (mu, w0, wd, a0, wa, wg, k_k, k_a, r_k, gn_w, gn_b)
    return pl.pallas_call(
        kern,
        grid=(b // nb,),
        in_specs=[pl.BlockSpec((nb, t, RW_COLS_P), lambda i: (i, 0, 0))] + [_const_spec(a.shape) for a in consts],
        out_specs=pl.BlockSpec((nb, t, RW_W), lambda i: (i, 0, 0)),
        out_shape=jax.ShapeDtypeStruct((b, t, RW_W), BF16),
        scratch_shapes=[pltpu.VMEM((nb, 2, RW_W // LANES, LANES, LANES), F32)],
        compiler_params=_params("arbitrary"),
        name="rwkv7",
    )(pr, *consts)


def _post_kernel(x_ref, a_ref, n_ref, c_ref, mod_ref, wo_ref, w13_ref, w2_ref, l1w_ref, l1b_ref, l2w_ref, l2b_ref,
                 o_ref, *, alpha, hidden, n_split):
    nb, tm, _ = x_ref.shape
    stack = lambda ref: jnp.concatenate([_bf(ref[i]) for i in range(nb)], axis=0)
    y = (_dot(stack(a_ref), wo_ref[0:GLA_V, :])
         + _dot(stack(n_ref), wo_ref[GLA_V:GLA_V + NAT_W, :])
         + _dot(stack(c_ref), wo_ref[GLA_V + NAT_W:, :]))
    x1 = [_layer_norm(alpha * x_ref[i] + mod_ref[i, 0, 2:3, :] * y[i * tm:(i + 1) * tm], l1w_ref[...], l1b_ref[...])
          for i in range(nb)]
    h = jnp.concatenate([_bf(x1[i] * (1.0 + mod_ref[i, 0, 4:5, :]) + mod_ref[i, 0, 3:4, :]) for i in range(nb)],
                        axis=0)
    hs = hidden // n_split
    f = None
    for k in range(n_split):
        gte = _dot(h, w13_ref[:, k * hs:(k + 1) * hs])
        up = _dot(h, w13_ref[:, hidden + k * hs:hidden + (k + 1) * hs])
        part = _dot(_bf(_silu(gte) * up), w2_ref[k * hs:(k + 1) * hs, :])
        f = part if f is None else f + part
    for i in range(nb):
        o_ref[i] = _layer_norm(alpha * x1[i] + mod_ref[i, 0, 5:6, :] * f[i * tm:(i + 1) * tm],
                               l2w_ref[...], l2b_ref[...])


def _post(xz, ya, yn, yc, modtab, w_out, w13, w2, ln1, ln2, tm, nct, row_off, alpha):
    b, t, d = xz.shape
    nb = DENSE_BATCH
    t_out = t - row_off * tm
    row = lambda bi, j: (bi, j + row_off, 0)
    seg = lambda bi, j: (bi, jnp.where(j + row_off >= nct, 1, 0), 0, 0)
    consts = (w_out, w13, w2, *ln1, *ln2)
    return pl.pallas_call(
        functools.partial(_post_kernel, alpha=alpha, hidden=w2.shape[0], n_split=2),
        grid=(b // nb, t_out // tm),
        in_specs=[pl.BlockSpec((nb, tm, d), row),
                  pl.BlockSpec((nb, tm, GLA_V), row),
                  pl.BlockSpec((nb, tm, NAT_W), row),
                  pl.BlockSpec((nb, tm, RW_W), row),
                  pl.BlockSpec((nb, 1, 6, d), seg)] + [_const_spec(a.shape) for a in consts],
        out_specs=pl.BlockSpec((nb, tm, d), lambda bi, j: (bi, j, 0)),
        out_shape=jax.ShapeDtypeStruct((b, t_out, d), F32),
        compiler_params=_params("arbitrary", "arbitrary"),
        name="outproj_swiglu",
    )(xz, ya, yn, yc, modtab, *consts)


def _rope_tables(lc, l):
    t = jnp.arange(l, dtype=jnp.int32)
    row = (t // GRID_W).astype(F32)
    col = (t % GRID_W).astype(F32)
    n_freq = GLA_DK // 4
    inv = ROPE_BASE ** (-jnp.arange(n_freq, dtype=F32) / n_freq)
    ang = jnp.concatenate([row[:, None] * inv, col[:, None] * inv], axis=-1)
    pair = (np.arange(LANES) % GLA_DK) // 2
    sign = np.where(np.arange(LANES) % 2 == 0, -1.0, 1.0).astype(np.float32)
    cos_l = jnp.cos(ang)[:, pair]
    sin_l = jnp.sin(ang)[:, pair] * sign
    cos_t = jnp.concatenate([jnp.ones((lc, LANES), F32), cos_l], axis=0)
    sin_t = jnp.concatenate([jnp.zeros((lc, LANES), F32), sin_l], axis=0)
    return cos_t, sin_t


def _nat_bias_table(rpb):
    KR, KC, W = NAT_WIN_R, NAT_WIN_C, GRID_W
    cq = np.arange(W)
    col0 = np.clip(cq - KC // 2, 0, W - KC)
    in_win = (cq[None, :] >= col0[:, None]) & (cq[None, :] < col0[:, None] + KC)
    dc = np.clip(cq[None, :] - cq[:, None], -(KC - 1), KC - 1) + KC - 1
    delta = np.arange(KR)
    dr = np.arange(KR)[None, :] - delta[:, None] + NAT_WIN_R - 1
    tab = rpb[:, dr][:, :, :, dc]
    tab = jnp.where(in_win[None, None, None], tab, -jnp.inf)
    tab = tab.transpose(0, 1, 3, 2, 4)
    return tab.reshape(NAT_HEADS, KR, W, KR * W)


def _pad_cols(w, width):
    return jnp.pad(w, ((0, 0), (0, width - w.shape[1])))


def _layer_params(i, w_in, gla_gate_up, gla_gate_b, gla_norm_w, nat_rpb, rw_mu, rw_w0, rw_wd2, rw_a0, rw_wa2,
                  rw_wg2, rw_k_k, rw_k_a, rw_r_k, rw_gn_w, rw_gn_b, w_out, ln1_w, ln1_b, ffn_w13, ffn_w2,
                  ln2_w, ln2_b):
    wi = w_in[i]
    o = 0
    pieces = []
    for size, width in ((GLA_QK, GLA_QK_PAD), (GLA_QK, GLA_QK_PAD), (GLA_V, GLA_V), (GLA_V, GLA_V),
                        (2 * GLA_GATE_RANK, LANES)):
        pieces.append(_pad_cols(wi[:, o:o + size], width))
        o += size
    pieces.append(wi[:, o:])
    p = {"w_in": _bf(jnp.concatenate(pieces, axis=1))}

    wg = jnp.zeros((LANES, 2 * GLA_QK_PAD), F32)
    wg = wg.at[0:GLA_GATE_RANK, 0:GLA_QK].set(gla_gate_up[i, 0])
    wg = wg.at[GLA_GATE_RANK:2 * GLA_GATE_RANK, GLA_QK_PAD:GLA_QK_PAD + GLA_QK].set(gla_gate_up[i, 1])
    p["gla_wg"] = _bf(wg)
    p["gla_gb"] = jnp.concatenate([_pad_cols(gla_gate_b[i, 0:1], GLA_QK_PAD),
                                   _pad_cols(gla_gate_b[i, 1:2], GLA_QK_PAD)], axis=1)
    p["gla_nw"] = jnp.tile(gla_norm_w[i], GLA_HEADS).reshape(1, GLA_V)
    p["nat_bias"] = _nat_bias_table(nat_rpb[i])

    def lora_pair(w, rank):
        z = jnp.zeros((2, LANES, RW_W), F32)
        z = z.at[0, 0:rank].set(w[0])
        return z.at[1, rank:2 * rank].set(w[1])

    row = lambda a: a.reshape(1, -1)
    p["rw"] = (row(rw_mu[i]), rw_w0[i].reshape(2, 1, RW_W), _bf(lora_pair(rw_wd2[i], RW_DECAY_RANK)),
               rw_a0[i].reshape(2, 1, RW_W), _bf(lora_pair(rw_wa2[i], RW_A_RANK)), _bf(rw_wg2[i]),
               row(rw_k_k[i]), row(rw_k_a[i]), row(rw_r_k[i]), row(rw_gn_w[i]), row(rw_gn_b[i]))
    p["w_out"] = _bf(w_out[i])
    p["ln1"] = (row(ln1_w[i]), row(ln1_b[i]))
    p["w13"] = _bf(ffn_w13[i])
    p["w2"] = _bf(ffn_w2[i])
    p["ln2"] = (row(ln2_w[i]), row(ln2_b[i]))
    return p


def kernel(x, c, ctx, c_ctx, w_mod, b_mod, w_in, gla_gate_up, gla_gate_b, gla_norm_w, nat_rpb, rw_mu, rw_w0,
           rw_wd2, rw_a0, rw_wa2, rw_wg2, rw_k_k, rw_k_a, rw_r_k, rw_gn_w, rw_gn_b, w_out, ln1_w, ln1_b,
           ffn_w13, ffn_w2, ln2_w, ln2_b):
    b, l, d = x.shape
    lc = ctx.shape[1]
    depth = w_mod.shape[0]
    alpha = (2.0 * depth) ** 0.25
    tm = 256 if lc % 256 == 0 else 128
    assert lc % tm == 0 and l % tm == 0 and lc % (2 * CHUNK) == 0 and l % (2 * CHUNK) == 0
    assert l % (NAT_ROWS_PER_STEP * GRID_W) == 0 and l // GRID_W >= NAT_WIN_R
    assert b % DENSE_BATCH == 0 and b % MIXER_BATCH == 0 and b % GLA_BATCH == 0
    nct = lc // tm

    cos_t, sin_t = _rope_tables(lc, l)
    rows = -(-(b + 1) // 8) * 8
    cvec = jnp.concatenate([c, c_ctx[None], jnp.zeros((rows - b - 1, d), F32)], axis=0)
    xz = jnp.concatenate([ctx, x], axis=1)

    for i in range(depth):
        last = i == depth - 1
        p = _layer_params(i, w_in, gla_gate_up, gla_gate_b, gla_norm_w, nat_rpb, rw_mu, rw_w0, rw_wd2, rw_a0,
                          rw_wa2, rw_wg2, rw_k_k, rw_k_a, rw_r_k, rw_gn_w, rw_gn_b, w_out, ln1_w, ln1_b,
                          ffn_w13, ffn_w2, ln2_w, ln2_b)
        m = _modulation(cvec, w_mod[i], b_mod[i])
        ml = m[:b].reshape(b, 1, 6, d)
        mc = jnp.broadcast_to(m[b].reshape(1, 1, 6, d), (b, 1, 6, d))
        modtab = jnp.concatenate([mc, ml], axis=1)

        pg, pn, pr = _inproj(xz, modtab, p["w_in"], cos_t, sin_t, tm, nct)
        ya = _gla(pg, p["gla_wg"], p["gla_gb"], p["gla_nw"], lc)
        yn = _nat(pn, p["nat_bias"], lc)
        yc = _rwkv(pr, lc, *p["rw"])
        row_off = nct if last else 0
        xz = _post(xz, ya, yn, yc, modtab, p["w_out"], p["w13"], p["w2"], p["ln1"], p["ln2"], tm, nct, row_off,
                   alpha)
    return xz
```

```python
import functools
import math

import jax
import jax.numpy as jnp
import numpy as np
from jax import lax
from jax.experimental import pallas as pl
from jax.experimental.pallas import tpu as pltpu

F32 = jnp.float32
BF16 = jnp.bfloat16

GRID_W = 64
GLA_HEADS, GLA_DK, GLA_DV, GLA_GATE_RANK, GLA_TAU = 6, 32, 64, 16, 16.0
NAT_HEADS, NAT_DH, NAT_WIN_R, NAT_WIN_C = 4, 64, 8, 16
RW_HEADS, RW_DH, RW_DECAY_RANK, RW_A_RANK, RW_GATE_RANK = 6, 64, 64, 64, 128
RW_GN_EPS = 64e-5
RW_DECAY_SCALE = math.exp(-0.5)
ROPE_BASE = 10000.0
LN_EPS = 1e-5

GLA_QK = GLA_HEADS * GLA_DK
GLA_V = GLA_HEADS * GLA_DV
NAT_W = NAT_HEADS * NAT_DH
RW_W = RW_HEADS * RW_DH

LANES = 128
CHUNK = 64
GLA_QK_PAD = 256
GLA_COLS_P = 2 * GLA_QK_PAD + 2 * GLA_V + LANES
NAT_COLS_P = 3 * NAT_W
RW_COLS_P = 3 * RW_W + 3 * LANES
VMEM_LIMIT = 56 * 1024 * 1024
DENSE_BATCH = 2
MIXER_BATCH = 2
GLA_BATCH = 4
NAT_ROWS_PER_STEP = 8

NN = (((1,), (0,)), ((), ()))
NT = (((1,), (1,)), ((), ()))
TN = (((0,), (0,)), ((), ()))


def _dot(a, b, dims=NN):
    return lax.dot_general(a, b, dims, preferred_element_type=F32)


def _bf(x):
    return x.astype(BF16)


def _split(x):
    hi = x.astype(BF16)
    lo = (x - hi.astype(F32)).astype(BF16)
    return hi, lo


def _dot3(a, b, dims=NN):
    ah, al = _split(a)
    bh, bl = _split(b)
    return _dot(ah, bh, dims) + (_dot(ah, bl, dims) + _dot(al, bh, dims))


def _dot_exact_lhs(a_bf, b, dims=NN):
    bh, bl = _split(b)
    return _dot(a_bf, bh, dims) + _dot(a_bf, bl, dims)


def _head_sums(x, ones_bd):
    return jnp.concatenate([_dot(_bf(x[:, j:j + LANES]), ones_bd) for j in range(0, x.shape[1], LANES)], axis=1)


def _sigmoid(x):
    return 1.0 / (1.0 + jnp.exp(-x))


def _silu(x):
    return x * _sigmoid(x)


def _iota(shape, axis):
    return lax.broadcasted_iota(jnp.int32, shape, axis)


def _div_pow2(x, n):
    return lax.shift_right_logical(x, int(math.log2(n)))


def _mod_pow2(x, n):
    return lax.bitwise_and(x, n - 1)


def _layer_norm(h, w, b):
    mu = jnp.mean(h, axis=-1, keepdims=True)
    xc = h - mu
    var = jnp.mean(xc * xc, axis=-1, keepdims=True)
    return xc * lax.rsqrt(var + LN_EPS) * w + b


def _params(*sem):
    return pltpu.CompilerParams(dimension_semantics=sem, vmem_limit_bytes=VMEM_LIMIT)


def _const_spec(shape):
    nd = len(shape)
    return pl.BlockSpec(shape, lambda *_: (0,) * nd, pipeline_mode=pl.Buffered(1))


def _mod_kernel(c_ref, w_ref, b_ref, o_ref):
    o_ref[...] = _dot3(_silu(c_ref[...]), w_ref[...]) + b_ref[...]


def _modulation(cvec, w_mod, b_mod):
    rows, d = cvec.shape
    n = w_mod.shape[1]
    tn = 1024
    return pl.pallas_call(
        _mod_kernel,
        grid=(n // tn,),
        in_specs=[pl.BlockSpec((rows, d), lambda j: (0, 0)),
                  pl.BlockSpec((d, tn), lambda j: (0, j)),
                  pl.BlockSpec((1, tn), lambda j: (0, j))],
        out_specs=pl.BlockSpec((rows, tn), lambda j: (0, j)),
        out_shape=jax.ShapeDtypeStruct((rows, n), F32),
        compiler_params=_params("arbitrary"),
        name="modulation",
    )(cvec, w_mod, b_mod.reshape(1, n))


def _token_specs(xs, nb, tm, nct, row_off):
    d = xs[0].shape[-1]
    if len(xs) == 1:
        return [pl.BlockSpec((nb, tm, d), lambda bi, j: (bi, j + row_off, 0))]
    assert row_off == 0
    return [pl.BlockSpec((nb, tm, d), lambda bi, j: (bi, jnp.minimum(j, nct - 1), 0)),
            pl.BlockSpec((nb, tm, d), lambda bi, j: (bi, jnp.maximum(j - nct, 0), 0))]


def _token_rows(x_refs, i, nct):
    if len(x_refs) == 1:
        return x_refs[0][i]
    return jnp.where(pl.program_id(1) < nct, x_refs[0][i], x_refs[1][i])


def _inproj_kernel(*refs, n_x, nct):
    x_refs = refs[:n_x]
    mod_ref, w_ref, cos_ref, sin_ref, og_ref, on_ref, or_ref = refs[n_x:]
    nb, tm, _ = x_refs[0].shape
    xm = jnp.concatenate([_bf(_token_rows(x_refs, i, nct) * (1.0 + mod_ref[i, 0, 1:2, :]) + mod_ref[i, 0, 0:1, :])
                          for i in range(nb)], axis=0)
    c0, c1 = GLA_COLS_P, GLA_COLS_P + NAT_COLS_P
    qk = _dot(xm, w_ref[:, 0:2 * GLA_QK_PAD])
    rest = [_dot(xm, w_ref[:, cols]) for cols in (slice(2 * GLA_QK_PAD, c0), slice(c0, c1), slice(c1, None))]
    even_lane = _mod_pow2(_iota((1, LANES), 1), 2) == 0
    cs = jnp.concatenate([cos_ref[...]] * nb, axis=0)
    sn = jnp.concatenate([sin_ref[...]] * nb, axis=0)
    for j in range(2 * GLA_QK_PAD // LANES):
        xb = qk[:, j * LANES:(j + 1) * LANES]
        sw = jnp.where(even_lane, pltpu.roll(xb, LANES - 1, 1), pltpu.roll(xb, 1, 1))
        rot = _bf(xb * cs + sw * sn)
        for i in range(nb):
            og_ref[i, :, j * LANES:(j + 1) * LANES] = rot[i * tm:(i + 1) * tm]
    for o_ref, col0, p in ((og_ref, 2 * GLA_QK_PAD, rest[0]), (on_ref, 0, rest[1]), (or_ref, 0, rest[2])):
        p = _bf(p)
        for i in range(nb):
            o_ref[i, :, col0:] = p[i * tm:(i + 1) * tm]


def _inproj(xs, modtab, w_in_p, cos_t, sin_t, tm, nct):
    b, _, d = xs[0].shape
    t = sum(a.shape[1] for a in xs)
    nb = DENSE_BATCH
    ncols = w_in_p.shape[1]
    seg = lambda bi, j: (bi, jnp.where(j >= nct, 1, 0), 0, 0)
    row = lambda bi, j: (bi, j, 0)
    return pl.pallas_call(
        functools.partial(_inproj_kernel, n_x=len(xs), nct=nct),
        grid=(b // nb, t // tm),
        in_specs=_token_specs(xs, nb, tm, nct, 0)
        + [pl.BlockSpec((nb, 1, 6, d), seg),
                  _const_spec((d, ncols)),
                  pl.BlockSpec((tm, LANES), lambda bi, j: (j, 0)),
                  pl.BlockSpec((tm, LANES), lambda bi, j: (j, 0))],
        out_specs=[pl.BlockSpec((nb, tm, GLA_COLS_P), row),
                   pl.BlockSpec((nb, tm, NAT_COLS_P), row),
                   pl.BlockSpec((nb, tm, RW_COLS_P), row)],
        out_shape=[jax.ShapeDtypeStruct((b, t, GLA_COLS_P), BF16),
                   jax.ShapeDtypeStruct((b, t, NAT_COLS_P), BF16),
                   jax.ShapeDtypeStruct((b, t, RW_COLS_P), BF16)],
        compiler_params=_params("arbitrary", "arbitrary"),
        name="inproj",
    )(*xs, modtab, w_in_p, cos_t, sin_t)


def _tri_ones(direction):
    r = _iota((CHUNK, CHUNK), 0)
    c = _iota((CHUNK, CHUNK), 1)
    keep = (c <= r) if direction == 0 else (c >= r)
    return jnp.where(keep, 1.0, 0.0).astype(BF16)


def _block_ones(width, block):
    r = _div_pow2(_iota((width, width), 0), block)
    c = _div_pow2(_iota((width, width), 1), block)
    return jnp.where(r == c, 1.0, 0.0).astype(BF16)


def _stages(streams):
    streams = list(streams)
    while streams:
        alive = []
        for s in streams:
            try:
                next(s)
                alive.append(s)
            except StopIteration:
                pass
        streams = alive
        yield


def _lockstep(streams):
    for _ in _stages(streams):
        pass


def _run_two_way(chunk_fn, c_lo, n, n_batch):
    half = n // 2

    def first(i, carry):
        _lockstep([s for bi in range(n_batch) for s in (chunk_fn(bi, c_lo + i, 0, True, c_lo, n),
                                                        chunk_fn(bi, c_lo + n - 1 - i, 1, True, c_lo, n))])
        return carry

    def second(i, carry):
        _lockstep([s for bi in range(n_batch) for s in (chunk_fn(bi, c_lo + half + i, 0, False, c_lo, n),
                                                        chunk_fn(bi, c_lo + half - 1 - i, 1, False, c_lo, n))])
        return carry

    lax.fori_loop(0, half, first, 0)
    lax.fori_loop(0, half, second, 0)


def _gla_kernel(pg_ref, wg_ref, gb_ref, nw_ref, o_ref, s_ref, *, n_ctx, n_all):
    C = CHUNK
    s_ref[...] = jnp.zeros(s_ref.shape, F32)

    lane_k = _iota((1, GLA_QK_PAD), 1)
    lane_v = _iota((1, GLA_V), 1)
    head_k = [_div_pow2(lane_k, GLA_DK) == h for h in range(GLA_HEADS)]
    head_v = [_div_pow2(lane_v, GLA_DV) == h for h in range(GLA_HEADS)]
    rows = _iota((C, GLA_HEADS * C), 0)
    cols = _mod_pow2(_iota((C, GLA_HEADS * C), 1), C)
    causal = [cols <= rows, cols >= rows]
    tri = [_tri_ones(0), _tri_ones(1)]
    bd_v = _block_ones(LANES, GLA_DV)
    s_rows = _div_pow2(_iota((GLA_V, GLA_QK_PAD), 0), GLA_DV)
    s_cols = _div_pow2(_iota((GLA_V, GLA_QK_PAD), 1), GLA_DK)
    s_diag = s_rows == s_cols
    scale = GLA_DK ** -0.5

    def chunk(bi, c, direction, is_first, c_lo, n):
        del c_lo, n
        r0 = pl.multiple_of(c * C, C)
        rs = pl.ds(r0, C)
        q = pg_ref[bi, rs, 0:GLA_QK_PAD].astype(F32)
        k = pg_ref[bi, rs, GLA_QK_PAD:2 * GLA_QK_PAD].astype(F32)
        v = pg_ref[bi, rs, 2 * GLA_QK_PAD:2 * GLA_QK_PAD + GLA_V]
        dn = pg_ref[bi, rs, 2 * GLA_QK_PAD + 2 * GLA_V:]
        wcols = slice(direction * GLA_QK_PAD, (direction + 1) * GLA_QK_PAD)
        z = _dot(dn, wg_ref[:, wcols]) + gb_ref[:, wcols]
        yield
        logd = jax.nn.log_sigmoid(z) * (1.0 / GLA_TAU)
        bcum = _dot_exact_lhs(tri[direction], logd)
        yield
        b_end = bcum[C - 1:C, :] if direction == 0 else bcum[0:1, :]
        q_e = _bf(q * jnp.exp(bcum) * scale)
        k_e = _bf(k * jnp.exp(-bcum))
        k_l = _bf(k * jnp.exp(b_end - bcum))

        k_stack = jnp.concatenate([jnp.where(head_k[h], k_e, jnp.zeros_like(k_e)) for h in range(GLA_HEADS)],
                                  axis=0)
        att = _dot(q_e, k_stack, NT)
        state = s_ref[bi, direction]
        o_inter = _dot(q_e, _bf(state), NT)
        kv = _dot(v, k_l, TN)
        yield
        att = _bf(jnp.where(causal[direction], att, 0.0))
        v_bd = jnp.concatenate([jnp.where(head_v[h], v, jnp.zeros_like(v)) for h in range(GLA_HEADS)], axis=0)
        o = _dot(att, v_bd) + o_inter
        decay = jnp.exp(b_end)
        for h in range(GLA_HEADS):
            rh = slice(h * GLA_DV, (h + 1) * GLA_DV)
            ch = slice(h * GLA_DK // LANES * LANES, (h * GLA_DK // LANES + 1) * LANES)
            s_ref[bi, direction, rh, ch] = state[rh, ch] * decay[:, ch] + jnp.where(s_diag[rh, ch], kv[rh, ch], 0.0)
        yield

        if is_first:
            o_ref[bi, rs, :] = _bf(o)
        else:
            tot = o_ref[bi, rs, :].astype(F32) + o
            g = pg_ref[bi, rs, 2 * GLA_QK_PAD + GLA_V:2 * GLA_QK_PAD + 2 * GLA_V].astype(F32)
            ms = _head_sums(tot * tot, bd_v) * (1.0 / GLA_DV)
            yield
            o_ref[bi, rs, :] = _bf(tot * lax.rsqrt(ms + LN_EPS) * nw_ref[...] * _silu(g))

    n_batch = pg_ref.shape[0]
    _run_two_way(chunk, 0, n_ctx, n_batch)
    _run_two_way(chunk, n_ctx, n_all - n_ctx, n_batch)


def _gla(pg, wg, gb, nw, lc):
    b, t, _ = pg.shape
    nb = GLA_BATCH
    kern = functools.partial(_gla_kernel, n_ctx=lc // CHUNK, n_all=t // CHUNK)
    return pl.pallas_call(
        kern,
        grid=(b // nb,),
        in_specs=[pl.BlockSpec((nb, t, GLA_COLS_P), lambda i: (i, 0, 0), pipeline_mode=pl.Buffered(1)),
                  _const_spec(wg.shape), _const_spec(gb.shape), _const_spec(nw.shape)],
        out_specs=pl.BlockSpec((nb, t, GLA_V), lambda i: (i, 0, 0)),
        out_shape=jax.ShapeDtypeStruct((b, t, GLA_V), BF16),
        scratch_shapes=[pltpu.VMEM((nb, 2, GLA_V, GLA_QK_PAD), F32)],
        compiler_params=_params("arbitrary"),
        name="gla",
    )(pg, wg, gb, nw)


def _nat_kernel(pn_ref, bias_ref, o_ref, *, lc, n_rows):
    W = GRID_W
    KR = NAT_WIN_R
    scale = NAT_DH ** -0.5
    half = [_iota((1, LANES), 1) < NAT_DH, _iota((1, LANES), 1) >= NAT_DH]
    n_pairs = NAT_W // LANES

    def col(group, j):
        return slice(group * NAT_W + j * LANES, group * NAT_W + (j + 1) * LANES)

    def stacked_queries(q_rows, j):
        q = pn_ref[0, q_rows, col(0, j)] * scale
        return _bf(jnp.concatenate([jnp.where(half[0], q, 0.0), jnp.where(half[1], q, 0.0)], axis=0))

    def unstack(o, nq):
        return _bf(jnp.where(half[0], o[0:nq], o[nq:]))

    def ctx_stream(j):
        s = _dot(stacked_queries(pl.ds(0, lc), j), _bf(pn_ref[0, 0:lc, col(1, j)]), NT)
        yield
        p = jnp.exp(s - s.max(axis=-1, keepdims=True))
        o = _dot(_bf(p), _bf(pn_ref[0, 0:lc, col(2, j)]))
        yield
        o_ref[0, 0:lc, col(0, j)] = unstack(o / p.sum(axis=-1, keepdims=True), lc)

    _lockstep([ctx_stream(j) for j in range(n_pairs)])

    def group_stream(j, i):
        n_g = NAT_ROWS_PER_STEP
        nq = 2 * W
        q_rows, qms, kbs, vbs, biases = [], [], [], [], []
        for g in range(n_g):
            r = n_g * i + g
            start = jnp.clip(r - KR // 2, 0, n_rows - KR)
            delta = r - start
            q_rows.append(pl.ds(pl.multiple_of(lc + r * W, W), W))
            k_rows = pl.ds(pl.multiple_of(lc + start * W, W), KR * W)
            qms.append(stacked_queries(q_rows[g], j))
            kbs.append(_bf(pn_ref[0, k_rows, col(1, j)]))
            vbs.append(_bf(pn_ref[0, k_rows, col(2, j)]))
            biases.append(jnp.concatenate([bias_ref[2 * j, delta], bias_ref[2 * j + 1, delta]], axis=0))
        s_loc = [_dot(qms[g], kbs[g], NT) for g in range(n_g)]
        s_ctx = _dot(jnp.concatenate(qms, axis=0), _bf(pn_ref[0, 0:lc, col(1, j)]), NT)
        yield
        p_ctx, o_loc, denom = [], [], []
        for g in range(n_g):
            sl = s_loc[g] + biases[g]
            sc = s_ctx[g * nq:(g + 1) * nq]
            m = jnp.maximum(sl.max(axis=-1, keepdims=True), sc.max(axis=-1, keepdims=True))
            pl_ = jnp.exp(sl - m)
            pc = jnp.exp(sc - m)
            denom.append(pl_.sum(axis=-1, keepdims=True) + pc.sum(axis=-1, keepdims=True))
            p_ctx.append(_bf(pc))
            o_loc.append(_dot(_bf(pl_), vbs[g]))
        o_ctx = _dot(jnp.concatenate(p_ctx, axis=0), _bf(pn_ref[0, 0:lc, col(2, j)]))
        yield
        for g in range(n_g):
            o = (o_loc[g] + o_ctx[g * nq:(g + 1) * nq]) / denom[g]
            o_ref[0, q_rows[g], col(0, j)] = unstack(o, W)

    def row_body(i, carry):
        _lockstep([group_stream(j, i) for j in range(n_pairs)])
        return carry

    lax.fori_loop(0, n_rows // NAT_ROWS_PER_STEP, row_body, 0)


def _nat(pn, bias_tab, lc):
    b, t, _ = pn.shape
    kern = functools.partial(_nat_kernel, lc=lc, n_rows=(t - lc) // GRID_W)
    return pl.pallas_call(
        kern,
        grid=(b,),
        in_specs=[pl.BlockSpec((1, t, NAT_COLS_P), lambda i: (i, 0, 0)),
                  _const_spec(bias_tab.shape)],
        out_specs=pl.BlockSpec((1, t, NAT_W), lambda i: (i, 0, 0)),
        out_shape=jax.ShapeDtypeStruct((b, t, NAT_W), BF16),
        compiler_params=_params("arbitrary"),
        name="nat",
    )(pn, bias_tab)


def _rw_kernel(pr_ref, mu_ref, w0_ref, wd_ref, a0_ref, wa_ref, wg_ref, kk_ref, ka_ref, rk_ref,
               gnw_ref, gnb_ref, o_ref, s_ref, *, n_ctx, n_all):
    C = CHUNK
    C2 = 2 * C
    n_pairs = RW_W // LANES
    t_all = n_all * C
    s_ref[...] = jnp.zeros(s_ref.shape, F32)

    lane = _iota((1, LANES), 1)
    half = [lane < RW_DH, lane >= RW_DH]
    row_id = _iota((C, 1), 0)
    tri = [_tri_ones(0), _tri_ones(1)]
    bd = _block_ones(LANES, RW_DH)
    r_c = _iota((C, C2), 0)
    c_c = _mod_pow2(_iota((C, C2), 1), C)
    incl = [c_c <= r_c, c_c >= r_c]
    r_s = _iota((C2, C2), 0)
    c_s = _iota((C2, C2), 1)
    strict = [c_s < r_s, c_s > r_s]
    n_double = int(math.log2(C))
    top = _iota((C2, LANES), 0) < C
    st_half = top == (_iota((C2, LANES), 1) < RW_DH)

    def seg_sum(x):
        return _head_sums(x, bd)

    def lora(x, w_ref, direction):
        return _dot(_bf(x), w_ref[direction])

    def chunk(bi, c, direction, is_first, c_lo, n):
        r0 = pl.multiple_of(c * C, C)
        rs = pl.ds(r0, C)
        p = pr_ref[bi, rs, :].astype(F32)
        halo = 16
        prev_row = pr_ref[bi, pl.ds(pl.multiple_of(jnp.maximum(r0 - halo, 0), halo), halo), :].astype(F32)[
            halo - 1:halo, :]
        next_row = pr_ref[bi, pl.ds(pl.multiple_of(jnp.minimum(r0 + C, t_all - halo), halo), halo), :].astype(F32)[
            0:1, :]
        prev_row = jnp.where(c == c_lo, 0.0, prev_row)
        next_row = jnp.where(c == c_lo + n - 1, 0.0, next_row)
        up = jnp.where(row_id == 0, prev_row, pltpu.roll(p, 1, 0))
        dn = jnp.where(row_id == C - 1, next_row, pltpu.roll(p, C - 1, 0))
        y = p + (0.5 * (up + dn) - p) * mu_ref[...]

        r = y[:, 0:RW_W]
        k = y[:, RW_W:2 * RW_W]
        v = y[:, 2 * RW_W:3 * RW_W]
        dd = y[:, 3 * RW_W:3 * RW_W + LANES]
        ad = y[:, 3 * RW_W + LANES:3 * RW_W + 2 * LANES]

        other = 1 - direction
        d_lin = lora(jnp.tanh(dd), wd_ref, direction)
        a_lin = lora(ad, wa_ref, direction)
        kk = k * kk_ref[...]
        kk_ss = seg_sum(kk * kk)
        if not is_first:
            a_lin_o = lora(ad, wa_ref, other)
            gate = _dot(_bf(_sigmoid(y[:, 3 * RW_W + 2 * LANES:])), wg_ref[...])
        yield
        logw = -RW_DECAY_SCALE * _sigmoid(w0_ref[direction] + d_lin)
        asig = _sigmoid(a0_ref[direction] + a_lin)
        kk = kk * lax.rsqrt(kk_ss + 1e-12)
        kmod = k * (1.0 + (asig - 1.0) * ka_ref[...])
        gcum = _dot_exact_lhs(tri[direction], logw)
        if not is_first:
            kmod_o = k * (1.0 + (_sigmoid(a0_ref[other] + a_lin_o) - 1.0) * ka_ref[...])
            bonus_w = seg_sum(r * (kmod + kmod_o) * rk_ref[...])
        yield
        g_end = gcum[C - 1:C, :] if direction == 0 else gcum[0:1, :]
        e_pos = jnp.exp(gcum)
        e_neg = jnp.exp(-gcum)
        r_t = r * e_pos
        a_t = -kk * jnp.exp(gcum - logw)
        b_t = kk * asig * e_neg
        k_t = kmod * e_neg
        decay_end = jnp.exp(g_end)

        outs = [None] * n_pairs

        def pair(j):
            cl = slice(j * LANES, (j + 1) * LANES)

            def stack(x):
                xb = x[:, cl]
                return jnp.concatenate([jnp.where(half[0], xb, 0.0), jnp.where(half[1], xb, 0.0)], axis=0)

            a_st = stack(a_t)
            v_st = _bf(stack(v))
            rhs = _bf(jnp.concatenate([stack(b_t), stack(k_t)], axis=0))
            lhs = _bf(jnp.concatenate([r_t[:, cl], a_st], axis=0))
            sc = _dot(lhs, rhs, NT)
            yield
            a_rb = jnp.where(incl[direction], sc[0:C, 0:C2], 0.0)
            a_rk = jnp.where(incl[direction], sc[0:C, C2:], 0.0)
            n_bd = jnp.where(strict[direction], sc[C:, 0:C2], 0.0)
            a_ak = jnp.where(strict[direction], sc[C:, C2:], 0.0)
            y2 = _dot(_bf(a_ak), v_st)
            npow = _bf(n_bd)
            yield
            x = jnp.concatenate([a_st[0:C] + pltpu.roll(y2[0:C], RW_DH, 1),
                                 pltpu.roll(a_st[C:], RW_DH, 1) + y2[C:]], axis=0)
            for i in range(n_double):
                if i + 1 < n_double:
                    both = _dot(npow, jnp.concatenate([_bf(x), npow], axis=1))
                    yield
                    x = x + both[:, 0:LANES]
                    npow = _bf(both[:, LANES:])
                else:
                    upd = _dot(npow, _bf(x))
                    yield
                    x = x + upd
            x_sw = jnp.concatenate([pltpu.roll(x[0:C], RW_DH, 1), pltpu.roll(x[C:], RW_DH, 1)], axis=0)
            a_hat = jnp.where(st_half, jnp.where(top, x, x_sw), 0.0)
            u_hat = jnp.where(st_half, jnp.where(top, x_sw, x), 0.0)
            state = s_ref[bi, direction, j]
            m3 = _dot(_bf(jnp.concatenate([r_t[:, cl], a_hat], axis=0)), _bf(state), NT)
            yield
            u = m3[C:] + u_hat
            uv = jnp.concatenate([_bf(u), v_st], axis=0)
            o_j = _dot(_bf(jnp.concatenate([a_rb, a_rk], axis=1)), uv)
            s_new = _dot(uv, rhs, TN)
            yield
            outs[j] = m3[0:C] + o_j
            s_ref[bi, direction, j] = (state + s_new) * decay_end[:, cl]

        yield from _stages([pair(j) for j in range(n_pairs)])
        o = jnp.concatenate(outs, axis=1)

        if is_first:
            o_ref[bi, rs, :] = _bf(o)
        else:
            tot = o_ref[bi, rs, :].astype(F32) + o
            inv_n = 1.0 / RW_DH
            mean = seg_sum(tot) * inv_n
            yield
            xc = tot - mean
            var = seg_sum(xc * xc) * inv_n
            yield
            yn = xc * lax.rsqrt(var + RW_GN_EPS) * gnw_ref[...] + gnb_ref[...]
            o_ref[bi, rs, :] = _bf((yn + bonus_w * v) * gate)

    n_batch = pr_ref.shape[0]
    _run_two_way(chunk, 0, n_ctx, n_batch)
    _run_two_way(chunk, n_ctx, n_all - n_ctx, n_batch)


def _rwkv(pr, lc, mu, w0, wd, a0, wa, wg, k_k, k_a, r_k, gn_w, gn_b):
    b, t, _ = pr.shape
    nb = MIXER_BATCH
    kern = functools.partial(_rw_kernel, n_ctx=lc // CHUNK, n_all=t // CHUNK)
    consts = (mu, w0, wd, a0, wa, wg, k_k, k_a, r_k, gn_w, gn_b)
    return pl.pallas_call(
        kern,
        grid=(b // nb,),
        in_specs=[pl.BlockSpec((nb, t, RW_COLS_P), lambda i: (i, 0, 0))] + [_const_spec(a.shape) for a in consts],
        out_specs=pl.BlockSpec((nb, t, RW_W), lambda i: (i, 0, 0)),
        out_shape=jax.ShapeDtypeStruct((b, t, RW_W), BF16),
        scratch_shapes=[pltpu.VMEM((nb, 2, RW_W // LANES, LANES, LANES), F32)],
        compiler_params=_params("arbitrary"),
        name="rwkv7",
    )(pr, *consts)


def _post_kernel(*refs, n_x, nct, alpha, hidden, n_split):
    x_refs = refs[:n_x]
    a_ref, n_ref, c_ref, mod_ref, wo_ref, w13_ref, w2_ref, l1w_ref, l1b_ref, l2w_ref, l2b_ref, o_ref = refs[n_x:]
    nb, tm, _ = x_refs[0].shape
    stack = lambda ref: jnp.concatenate([_bf(ref[i]) for i in range(nb)], axis=0)
    y = (_dot(stack(a_ref), wo_ref[0:GLA_V, :])
         + _dot(stack(n_ref), wo_ref[GLA_V:GLA_V + NAT_W, :])
         + _dot(stack(c_ref), wo_ref[GLA_V + NAT_W:, :]))
    x1 = [_layer_norm(alpha * _token_rows(x_refs, i, nct) + mod_ref[i, 0, 2:3, :] * y[i * tm:(i + 1) * tm],
                      l1w_ref[...], l1b_ref[...]) for i in range(nb)]
    h = jnp.concatenate([_bf(x1[i] * (1.0 + mod_ref[i, 0, 4:5, :]) + mod_ref[i, 0, 3:4, :]) for i in range(nb)],
                        axis=0)
    hs = hidden // n_split
    f = None
    for k in range(n_split):
        gte = _dot(h, w13_ref[:, k * hs:(k + 1) * hs])
        up = _dot(h, w13_ref[:, hidden + k * hs:hidden + (k + 1) * hs])
        part = _dot(_bf(_silu(gte) * up), w2_ref[k * hs:(k + 1) * hs, :])
        f = part if f is None else f + part
    for i in range(nb):
        o_ref[i] = _layer_norm(alpha * x1[i] + mod_ref[i, 0, 5:6, :] * f[i * tm:(i + 1) * tm],
                               l2w_ref[...], l2b_ref[...])


def _post(xs, ya, yn, yc, modtab, w_out, w13, w2, ln1, ln2, tm, nct, row_off, alpha):
    b, _, d = xs[0].shape
    t = sum(a.shape[1] for a in xs)
    nb = DENSE_BATCH
    t_out = t - row_off * tm
    row = lambda bi, j: (bi, j + row_off, 0)
    seg = lambda bi, j: (bi, jnp.where(j + row_off >= nct, 1, 0), 0, 0)
    consts = (w_out, w13, w2, *ln1, *ln2)
    return pl.pallas_call(
        functools.partial(_post_kernel, n_x=len(xs), nct=nct, alpha=alpha, hidden=w2.shape[0], n_split=2),
        grid=(b // nb, t_out // tm),
        in_specs=_token_specs(xs, nb, tm, nct, row_off)
        + [pl.BlockSpec((nb, tm, GLA_V), row),
                  pl.BlockSpec((nb, tm, NAT_W), row),
                  pl.BlockSpec((nb, tm, RW_W), row),
                  pl.BlockSpec((nb, 1, 6, d), seg)] + [_const_spec(a.shape) for a in consts],
        out_specs=pl.BlockSpec((nb, tm, d), lambda bi, j: (bi, j, 0)),
        out_shape=jax.ShapeDtypeStruct((b, t_out, d), F32),
        compiler_params=_params("arbitrary", "arbitrary"),
        name="outproj_swiglu",
    )(*xs, ya, yn, yc, modtab, *consts)


def _rope_tables(lc, l):
    t = jnp.arange(l, dtype=jnp.int32)
    row = (t // GRID_W).astype(F32)
    col = (t % GRID_W).astype(F32)
    n_freq = GLA_DK // 4
    inv = ROPE_BASE ** (-jnp.arange(n_freq, dtype=F32) / n_freq)
    ang = jnp.concatenate([row[:, None] * inv, col[:, None] * inv], axis=-1)
    pair = (np.arange(LANES) % GLA_DK) // 2
    sign = np.where(np.arange(LANES) % 2 == 0, -1.0, 1.0).astype(np.float32)
    cos_l = jnp.cos(ang)[:, pair]
    sin_l = jnp.sin(ang)[:, pair] * sign
    cos_t = jnp.concatenate([jnp.ones((lc, LANES), F32), cos_l], axis=0)
    sin_t = jnp.concatenate([jnp.zeros((lc, LANES), F32), sin_l], axis=0)
    return cos_t, sin_t


def _nat_bias_table(rpb):
    KR, KC, W = NAT_WIN_R, NAT_WIN_C, GRID_W
    cq = np.arange(W)
    col0 = np.clip(cq - KC // 2, 0, W - KC)
    in_win = (cq[None, :] >= col0[:, None]) & (cq[None, :] < col0[:, None] + KC)
    dc = np.clip(cq[None, :] - cq[:, None], -(KC - 1), KC - 1) + KC - 1
    delta = np.arange(KR)
    dr = np.arange(KR)[None, :] - delta[:, None] + NAT_WIN_R - 1
    tab = rpb[:, dr][:, :, :, dc]
    tab = jnp.where(in_win[None, None, None], tab, -jnp.inf)
    tab = tab.transpose(0, 1, 3, 2, 4)
    return tab.reshape(NAT_HEADS, KR, W, KR * W)


def _pad_cols(w, width):
    return jnp.pad(w, ((0, 0), (0, width - w.shape[1])))


def _layer_params(i, w_in, gla_gate_up, gla_gate_b, gla_norm_w, nat_rpb, rw_mu, rw_w0, rw_wd2, rw_a0, rw_wa2,
                  rw_wg2, rw_k_k, rw_k_a, rw_r_k, rw_gn_w, rw_gn_b, w_out, ln1_w, ln1_b, ffn_w13, ffn_w2,
                  ln2_w, ln2_b):
    wi = w_in[i]
    o = 0
    pieces = []
    for size, width in ((GLA_QK, GLA_QK_PAD), (GLA_QK, GLA_QK_PAD), (GLA_V, GLA_V), (GLA_V, GLA_V),
                        (2 * GLA_GATE_RANK, LANES)):
        pieces.append(_pad_cols(wi[:, o:o + size], width))
        o += size
    pieces.append(wi[:, o:])
    p = {"w_in": _bf(jnp.concatenate(pieces, axis=1))}

    wg = jnp.zeros((LANES, 2 * GLA_QK_PAD), F32)
    wg = wg.at[0:GLA_GATE_RANK, 0:GLA_QK].set(gla_gate_up[i, 0])
    wg = wg.at[GLA_GATE_RANK:2 * GLA_GATE_RANK, GLA_QK_PAD:GLA_QK_PAD + GLA_QK].set(gla_gate_up[i, 1])
    p["gla_wg"] = _bf(wg)
    p["gla_gb"] = jnp.concatenate([_pad_cols(gla_gate_b[i, 0:1], GLA_QK_PAD),
                                   _pad_cols(gla_gate_b[i, 1:2], GLA_QK_PAD)], axis=1)
    p["gla_nw"] = jnp.tile(gla_norm_w[i], GLA_HEADS).reshape(1, GLA_V)
    p["nat_bias"] = _nat_bias_table(nat_rpb[i])

    def lora_pair(w, rank):
        z = jnp.zeros((2, LANES, RW_W), F32)
        z = z.at[0, 0:rank].set(w[0])
        return z.at[1, rank:2 * rank].set(w[1])

    row = lambda a: a.reshape(1, -1)
    p["rw"] = (row(rw_mu[i]), rw_w0[i].reshape(2, 1, RW_W), _bf(lora_pair(rw_wd2[i], RW_DECAY_RANK)),
               rw_a0[i].reshape(2, 1, RW_W), _bf(lora_pair(rw_wa2[i], RW_A_RANK)), _bf(rw_wg2[i]),
               row(rw_k_k[i]), row(rw_k_a[i]), row(rw_r_k[i]), row(rw_gn_w[i]), row(rw_gn_b[i]))
    p["w_out"] = _bf(w_out[i])
    p["ln1"] = (row(ln1_w[i]), row(ln1_b[i]))
    p["w13"] = _bf(ffn_w13[i])
    p["w2"] = _bf(ffn_w2[i])
    p["ln2"] = (row(ln2_w[i]), row(ln2_b[i]))
    return p


def kernel(x, c, ctx, c_ctx, w_mod, b_mod, w_in, gla_gate_up, gla_gate_b, gla_norm_w, nat_rpb, rw_mu, rw_w0,
           rw_wd2, rw_a0, rw_wa2, rw_wg2, rw_k_k, rw_k_a, rw_r_k, rw_gn_w, rw_gn_b, w_out, ln1_w, ln1_b,
           ffn_w13, ffn_w2, ln2_w, ln2_b):
    b, l, d = x.shape
    lc = ctx.shape[1]
    depth = w_mod.shape[0]
    alpha = (2.0 * depth) ** 0.25
    tm = 256 if lc % 256 == 0 else 128
    assert lc % tm == 0 and l % tm == 0 and lc % (2 * CHUNK) == 0 and l % (2 * CHUNK) == 0
    assert l % (NAT_ROWS_PER_STEP * GRID_W) == 0 and l // GRID_W >= NAT_WIN_R
    assert b % DENSE_BATCH == 0 and b % MIXER_BATCH == 0 and b % GLA_BATCH == 0
    nct = lc // tm

    cos_t, sin_t = _rope_tables(lc, l)
    rows = -(-(b + 1) // 8) * 8
    cvec = jnp.concatenate([c, c_ctx[None], jnp.zeros((rows - b - 1, d), F32)], axis=0)
    xs = (ctx, x) if depth > 1 else (jnp.concatenate([ctx, x], axis=1),)

    for i in range(depth):
        last = i == depth - 1
        p = _layer_params(i, w_in, gla_gate_up, gla_gate_b, gla_norm_w, nat_rpb, rw_mu, rw_w0, rw_wd2, rw_a0,
                          rw_wa2, rw_wg2, rw_k_k, rw_k_a, rw_r_k, rw_gn_w, rw_gn_b, w_out, ln1_w, ln1_b,
                          ffn_w13, ffn_w2, ln2_w, ln2_b)
        m = _modulation(cvec, w_mod[i], b_mod[i])
        ml = m[:b].reshape(b, 1, 6, d)
        mc = jnp.broadcast_to(m[b].reshape(1, 1, 6, d), (b, 1, 6, d))
        modtab = jnp.concatenate([mc, ml], axis=1)

        pg, pn, pr = _inproj(xs, modtab, p["w_in"], cos_t, sin_t, tm, nct)
        ya = _gla(pg, p["gla_wg"], p["gla_gb"], p["gla_nw"], lc)
        yn = _nat(pn, p["nat_bias"], lc)
        yc = _rwkv(pr, lc, *p["rw"])
        row_off = nct if last else 0
        xs = (_post(xs, ya, yn, yc, modtab, p["w_out"], p["w13"], p["w2"], p["ln1"], p["ln2"], tm, nct, row_off,
                    alpha),)
    return xs[0]
```

```python
import functools
import math

import jax
import jax.numpy as jnp
import numpy as np
from jax import lax
from jax.experimental import pallas as pl
from jax.experimental.pallas import tpu as pltpu

F32 = jnp.float32
BF16 = jnp.bfloat16

GRID_W = 64
GLA_HEADS, GLA_DK, GLA_DV, GLA_GATE_RANK, GLA_TAU = 6, 32, 64, 16, 16.0
NAT_HEADS, NAT_DH, NAT_WIN_R, NAT_WIN_C = 4, 64, 8, 16
RW_HEADS, RW_DH, RW_DECAY_RANK, RW_A_RANK, RW_GATE_RANK = 6, 64, 64, 64, 128
RW_GN_EPS = 64e-5
RW_DECAY_SCALE = math.exp(-0.5)
ROPE_BASE = 10000.0
LN_EPS = 1e-5

GLA_QK = GLA_HEADS * GLA_DK
GLA_V = GLA_HEADS * GLA_DV
NAT_W = NAT_HEADS * NAT_DH
RW_W = RW_HEADS * RW_DH

LANES = 128
CHUNK = 64
GLA_QK_PAD = 256
GLA_COLS_P = 2 * GLA_QK_PAD + 2 * GLA_V + LANES
NAT_COLS_P = 3 * NAT_W
RW_COLS_P = 3 * RW_W + 3 * LANES
VMEM_LIMIT = 56 * 1024 * 1024
DENSE_BATCH = 2
MIXER_BATCH = 2
GLA_BATCH = 4
NAT_ROWS_PER_STEP = 8

NN = (((1,), (0,)), ((), ()))
NT = (((1,), (1,)), ((), ()))
TN = (((0,), (0,)), ((), ()))


def _dot(a, b, dims=NN):
    return lax.dot_general(a, b, dims, preferred_element_type=F32)


def _bf(x):
    return x.astype(BF16)


def _split(x):
    hi = x.astype(BF16)
    lo = (x - hi.astype(F32)).astype(BF16)
    return hi, lo


def _dot3(a, b, dims=NN):
    ah, al = _split(a)
    bh, bl = _split(b)
    return _dot(ah, bh, dims) + (_dot(ah, bl, dims) + _dot(al, bh, dims))


def _dot_exact_lhs(a_bf, b, dims=NN):
    bh, bl = _split(b)
    return _dot(a_bf, bh, dims) + _dot(a_bf, bl, dims)


def _head_sums(x, ones_bd):
    rows = x.shape[0]
    blocks = [_bf(x[:, j:j + LANES]) for j in range(0, x.shape[1], LANES)]
    s = _dot(jnp.concatenate(blocks, axis=0), ones_bd)
    return jnp.concatenate([s[i * rows:(i + 1) * rows] for i in range(len(blocks))], axis=1)


def _sigmoid(x):
    return 1.0 / (1.0 + jnp.exp(-x))


def _silu(x):
    return x * _sigmoid(x)


def _iota(shape, axis):
    return lax.broadcasted_iota(jnp.int32, shape, axis)


def _div_pow2(x, n):
    return lax.shift_right_logical(x, int(math.log2(n)))


def _mod_pow2(x, n):
    return lax.bitwise_and(x, n - 1)


def _layer_norm(h, w, b):
    mu = jnp.mean(h, axis=-1, keepdims=True)
    xc = h - mu
    var = jnp.mean(xc * xc, axis=-1, keepdims=True)
    return xc * lax.rsqrt(var + LN_EPS) * w + b


def _params(*sem):
    return pltpu.CompilerParams(dimension_semantics=sem, vmem_limit_bytes=VMEM_LIMIT)


def _const_spec(shape):
    nd = len(shape)
    return pl.BlockSpec(shape, lambda *_: (0,) * nd, pipeline_mode=pl.Buffered(1))


def _mod_kernel(c_ref, w_ref, b_ref, o_ref):
    o_ref[...] = _dot3(_silu(c_ref[...]), w_ref[...]) + b_ref[...]


def _modulation(cvec, w_mod, b_mod):
    rows, d = cvec.shape
    n = w_mod.shape[1]
    tn = 1024
    return pl.pallas_call(
        _mod_kernel,
        grid=(n // tn,),
        in_specs=[pl.BlockSpec((rows, d), lambda j: (0, 0)),
                  pl.BlockSpec((d, tn), lambda j: (0, j)),
                  pl.BlockSpec((1, tn), lambda j: (0, j))],
        out_specs=pl.BlockSpec((rows, tn), lambda j: (0, j)),
        out_shape=jax.ShapeDtypeStruct((rows, n), F32),
        compiler_params=_params("arbitrary"),
        name="modulation",
    )(cvec, w_mod, b_mod.reshape(1, n))


def _token_specs(xs, nb, tm, nct, row_off):
    d = xs[0].shape[-1]
    if len(xs) == 1:
        return [pl.BlockSpec((nb, tm, d), lambda bi, j: (bi, j + row_off, 0))]
    assert row_off == 0
    return [pl.BlockSpec((nb, tm, d), lambda bi, j: (bi, jnp.minimum(j, nct - 1), 0)),
            pl.BlockSpec((nb, tm, d), lambda bi, j: (bi, jnp.maximum(j - nct, 0), 0))]


def _token_rows(x_refs, i, nct):
    if len(x_refs) == 1:
        return x_refs[0][i]
    return jnp.where(pl.program_id(1) < nct, x_refs[0][i], x_refs[1][i])


def _inproj_kernel(*refs, n_x, nct):
    x_refs = refs[:n_x]
    mod_ref, w_ref, cos_ref, sin_ref, og_ref, on_ref, or_ref = refs[n_x:]
    nb, tm, _ = x_refs[0].shape
    xm = jnp.concatenate([_bf(_token_rows(x_refs, i, nct) * (1.0 + mod_ref[i, 0, 1:2, :]) + mod_ref[i, 0, 0:1, :])
                          for i in range(nb)], axis=0)
    c0, c1 = GLA_COLS_P, GLA_COLS_P + NAT_COLS_P
    qk = _dot(xm, w_ref[:, 0:2 * GLA_QK_PAD])
    rest = [_dot(xm, w_ref[:, cols]) for cols in (slice(2 * GLA_QK_PAD, c0), slice(c0, c1), slice(c1, None))]
    even_lane = _mod_pow2(_iota((1, LANES), 1), 2) == 0
    cs = jnp.concatenate([cos_ref[...]] * nb, axis=0)
    sn = jnp.concatenate([sin_ref[...]] * nb, axis=0)
    for j in range(2 * GLA_QK_PAD // LANES):
        xb = qk[:, j * LANES:(j + 1) * LANES]
        sw = jnp.where(even_lane, pltpu.roll(xb, LANES - 1, 1), pltpu.roll(xb, 1, 1))
        rot = _bf(xb * cs + sw * sn)
        for i in range(nb):
            og_ref[i, :, j * LANES:(j + 1) * LANES] = rot[i * tm:(i + 1) * tm]
    for o_ref, col0, p in ((og_ref, 2 * GLA_QK_PAD, rest[0]), (on_ref, 0, rest[1]), (or_ref, 0, rest[2])):
        p = _bf(p)
        for i in range(nb):
            o_ref[i, :, col0:] = p[i * tm:(i + 1) * tm]


def _inproj(xs, modtab, w_in_p, cos_t, sin_t, tm, nct):
    b, _, d = xs[0].shape
    t = sum(a.shape[1] for a in xs)
    nb = DENSE_BATCH
    ncols = w_in_p.shape[1]
    seg = lambda bi, j: (bi, jnp.where(j >= nct, 1, 0), 0, 0)
    row = lambda bi, j: (bi, j, 0)
    return pl.pallas_call(
        functools.partial(_inproj_kernel, n_x=len(xs), nct=nct),
        grid=(b // nb, t // tm),
        in_specs=_token_specs(xs, nb, tm, nct, 0)
        + [pl.BlockSpec((nb, 1, 6, d), seg),
                  _const_spec((d, ncols)),
                  pl.BlockSpec((tm, LANES), lambda bi, j: (j, 0)),
                  pl.BlockSpec((tm, LANES), lambda bi, j: (j, 0))],
        out_specs=[pl.BlockSpec((nb, tm, GLA_COLS_P), row),
                   pl.BlockSpec((nb, tm, NAT_COLS_P), row),
                   pl.BlockSpec((nb, tm, RW_COLS_P), row)],
        out_shape=[jax.ShapeDtypeStruct((b, t, GLA_COLS_P), BF16),
                   jax.ShapeDtypeStruct((b, t, NAT_COLS_P), BF16),
                   jax.ShapeDtypeStruct((b, t, RW_COLS_P), BF16)],
        compiler_params=_params("arbitrary", "arbitrary"),
        name="inproj",
    )(*xs, modtab, w_in_p, cos_t, sin_t)


def _tri_ones(direction):
    r = _iota((CHUNK, CHUNK), 0)
    c = _iota((CHUNK, CHUNK), 1)
    keep = (c <= r) if direction == 0 else (c >= r)
    return jnp.where(keep, 1.0, 0.0).astype(BF16)


def _block_ones(width, block):
    r = _div_pow2(_iota((width, width), 0), block)
    c = _div_pow2(_iota((width, width), 1), block)
    return jnp.where(r == c, 1.0, 0.0).astype(BF16)


def _stages(streams):
    streams = list(streams)
    while streams:
        alive = []
        for s in streams:
            try:
                next(s)
                alive.append(s)
            except StopIteration:
                pass
        streams = alive
        yield


def _lockstep(streams):
    for _ in _stages(streams):
        pass


def _run_two_way(chunk_fn, c_lo, n, n_batch):
    half = n // 2

    def first(i, carry):
        _lockstep([s for bi in range(n_batch) for s in (chunk_fn(bi, c_lo + i, 0, True, c_lo, n),
                                                        chunk_fn(bi, c_lo + n - 1 - i, 1, True, c_lo, n))])
        return carry

    def second(i, carry):
        _lockstep([s for bi in range(n_batch) for s in (chunk_fn(bi, c_lo + half + i, 0, False, c_lo, n),
                                                        chunk_fn(bi, c_lo + half - 1 - i, 1, False, c_lo, n))])
        return carry

    lax.fori_loop(0, half, first, 0)
    lax.fori_loop(0, half, second, 0)


def _gla_kernel(pg_ref, wg_ref, gb_ref, nw_ref, o_ref, s_ref, *, n_ctx, n_all):
    C = CHUNK
    s_ref[...] = jnp.zeros(s_ref.shape, F32)

    lane_k = _iota((1, GLA_QK_PAD), 1)
    lane_v = _iota((1, GLA_V), 1)
    head_k = [_div_pow2(lane_k, GLA_DK) == h for h in range(GLA_HEADS)]
    head_v = [_div_pow2(lane_v, GLA_DV) == h for h in range(GLA_HEADS)]
    rows = _iota((C, GLA_HEADS * C), 0)
    cols = _mod_pow2(_iota((C, GLA_HEADS * C), 1), C)
    causal = [cols <= rows, cols >= rows]
    tri = [_tri_ones(0), _tri_ones(1)]
    bd_v = _block_ones(LANES, GLA_DV)
    s_rows = _div_pow2(_iota((GLA_V, GLA_QK_PAD), 0), GLA_DV)
    s_cols = _div_pow2(_iota((GLA_V, GLA_QK_PAD), 1), GLA_DK)
    s_diag = s_rows == s_cols
    scale = GLA_DK ** -0.5

    def chunk(bi, c, direction, is_first, c_lo, n):
        del c_lo, n
        r0 = pl.multiple_of(c * C, C)
        rs = pl.ds(r0, C)
        q = pg_ref[bi, rs, 0:GLA_QK_PAD].astype(F32)
        k = pg_ref[bi, rs, GLA_QK_PAD:2 * GLA_QK_PAD].astype(F32)
        v = pg_ref[bi, rs, 2 * GLA_QK_PAD:2 * GLA_QK_PAD + GLA_V]
        dn = pg_ref[bi, rs, 2 * GLA_QK_PAD + 2 * GLA_V:]
        wcols = slice(direction * GLA_QK_PAD, (direction + 1) * GLA_QK_PAD)
        z = _dot(dn, wg_ref[:, wcols]) + gb_ref[:, wcols]
        yield
        logd = (jnp.minimum(z, 0.0) - jnp.log1p(jnp.exp(-jnp.abs(z)))) * (1.0 / GLA_TAU)
        bcum = _dot_exact_lhs(tri[direction], logd)
        yield
        b_end = bcum[C - 1:C, :] if direction == 0 else bcum[0:1, :]
        q_e = _bf(q * jnp.exp(bcum) * scale)
        k_e = _bf(k * jnp.exp(-bcum))
        k_l = _bf(k * jnp.exp(b_end - bcum))

        k_stack = jnp.concatenate([jnp.where(head_k[h], k_e, jnp.zeros_like(k_e)) for h in range(GLA_HEADS)],
                                  axis=0)
        att = _dot(q_e, k_stack, NT)
        state = s_ref[bi, direction]
        o_inter = _dot(q_e, _bf(state), NT)
        kv = _dot(v, k_l, TN)
        yield
        att = _bf(jnp.where(causal[direction], att, 0.0))
        v_bd = jnp.concatenate([jnp.where(head_v[h], v, jnp.zeros_like(v)) for h in range(GLA_HEADS)], axis=0)
        o = _dot(att, v_bd) + o_inter
        decay = jnp.exp(b_end)
        for h in range(GLA_HEADS):
            rh = slice(h * GLA_DV, (h + 1) * GLA_DV)
            ch = slice(h * GLA_DK // LANES * LANES, (h * GLA_DK // LANES + 1) * LANES)
            s_ref[bi, direction, rh, ch] = state[rh, ch] * decay[:, ch] + jnp.where(s_diag[rh, ch], kv[rh, ch], 0.0)
        yield

        if is_first:
            o_ref[bi, rs, :] = _bf(o)
        else:
            tot = o_ref[bi, rs, :].astype(F32) + o
            g = pg_ref[bi, rs, 2 * GLA_QK_PAD + GLA_V:2 * GLA_QK_PAD + 2 * GLA_V].astype(F32)
            ms = _head_sums(tot * tot, bd_v) * (1.0 / GLA_DV)
            yield
            o_ref[bi, rs, :] = _bf(tot * lax.rsqrt(ms + LN_EPS) * nw_ref[...] * _silu(g))

    n_batch = pg_ref.shape[0]
    _run_two_way(chunk, 0, n_ctx, n_batch)
    _run_two_way(chunk, n_ctx, n_all - n_ctx, n_batch)


def _gla(pg, wg, gb, nw, lc):
    b, t, _ = pg.shape
    nb = GLA_BATCH
    kern = functools.partial(_gla_kernel, n_ctx=lc // CHUNK, n_all=t // CHUNK)
    return pl.pallas_call(
        kern,
        grid=(b // nb,),
        in_specs=[pl.BlockSpec((nb, t, GLA_COLS_P), lambda i: (i, 0, 0), pipeline_mode=pl.Buffered(1)),
                  _const_spec(wg.shape), _const_spec(gb.shape), _const_spec(nw.shape)],
        out_specs=pl.BlockSpec((nb, t, GLA_V), lambda i: (i, 0, 0)),
        out_shape=jax.ShapeDtypeStruct((b, t, GLA_V), BF16),
        scratch_shapes=[pltpu.VMEM((nb, 2, GLA_V, GLA_QK_PAD), F32)],
        compiler_params=_params("arbitrary"),
        name="gla",
    )(pg, wg, gb, nw)


def _nat_kernel(pn_ref, bias_ref, o_ref, *, lc, n_rows):
    W = GRID_W
    KR = NAT_WIN_R
    scale = NAT_DH ** -0.5
    half = [_iota((1, LANES), 1) < NAT_DH, _iota((1, LANES), 1) >= NAT_DH]
    n_pairs = NAT_W // LANES

    def col(group, j):
        return slice(group * NAT_W + j * LANES, group * NAT_W + (j + 1) * LANES)

    def stacked_queries(q_rows, j):
        q = pn_ref[0, q_rows, col(0, j)] * scale
        return _bf(jnp.concatenate([jnp.where(half[0], q, 0.0), jnp.where(half[1], q, 0.0)], axis=0))

    def unstack(o, nq):
        return _bf(jnp.where(half[0], o[0:nq], o[nq:]))

    def ctx_stream(j):
        s = _dot(stacked_queries(pl.ds(0, lc), j), _bf(pn_ref[0, 0:lc, col(1, j)]), NT)
        yield
        p = jnp.exp(s - s.max(axis=-1, keepdims=True))
        o = _dot(_bf(p), _bf(pn_ref[0, 0:lc, col(2, j)]))
        yield
        o_ref[0, 0:lc, col(0, j)] = unstack(o / p.sum(axis=-1, keepdims=True), lc)

    _lockstep([ctx_stream(j) for j in range(n_pairs)])

    def group_stream(j, i):
        n_g = NAT_ROWS_PER_STEP
        nq = 2 * W
        q_rows, qms, kbs, vbs, biases = [], [], [], [], []
        for g in range(n_g):
            r = n_g * i + g
            start = jnp.clip(r - KR // 2, 0, n_rows - KR)
            delta = r - start
            q_rows.append(pl.ds(pl.multiple_of(lc + r * W, W), W))
            k_rows = pl.ds(pl.multiple_of(lc + start * W, W), KR * W)
            qms.append(stacked_queries(q_rows[g], j))
            kbs.append(_bf(pn_ref[0, k_rows, col(1, j)]))
            vbs.append(_bf(pn_ref[0, k_rows, col(2, j)]))
            biases.append(jnp.concatenate([bias_ref[2 * j, delta], bias_ref[2 * j + 1, delta]], axis=0))
        s_loc = [_dot(qms[g], kbs[g], NT) for g in range(n_g)]
        s_ctx = _dot(jnp.concatenate(qms, axis=0), _bf(pn_ref[0, 0:lc, col(1, j)]), NT)
        yield
        p_ctx, o_loc, denom = [], [], []
        for g in range(n_g):
            sl = s_loc[g] + biases[g]
            sc = s_ctx[g * nq:(g + 1) * nq]
            m = jnp.maximum(sl.max(axis=-1, keepdims=True), sc.max(axis=-1, keepdims=True))
            pl_ = jnp.exp(sl - m)
            pc = jnp.exp(sc - m)
            denom.append(pl_.sum(axis=-1, keepdims=True) + pc.sum(axis=-1, keepdims=True))
            p_ctx.append(_bf(pc))
            o_loc.append(_dot(_bf(pl_), vbs[g]))
        o_ctx = _dot(jnp.concatenate(p_ctx, axis=0), _bf(pn_ref[0, 0:lc, col(2, j)]))
        yield
        for g in range(n_g):
            o = (o_loc[g] + o_ctx[g * nq:(g + 1) * nq]) / denom[g]
            o_ref[0, q_rows[g], col(0, j)] = unstack(o, W)

    def row_body(i, carry):
        _lockstep([group_stream(j, i) for j in range(n_pairs)])
        return carry

    lax.fori_loop(0, n_rows // NAT_ROWS_PER_STEP, row_body, 0)


def _nat(pn, bias_tab, lc):
    b, t, _ = pn.shape
    kern = functools.partial(_nat_kernel, lc=lc, n_rows=(t - lc) // GRID_W)
    return pl.pallas_call(
        kern,
        grid=(b,),
        in_specs=[pl.BlockSpec((1, t, NAT_COLS_P), lambda i: (i, 0, 0)),
                  _const_spec(bias_tab.shape)],
        out_specs=pl.BlockSpec((1, t, NAT_W), lambda i: (i, 0, 0)),
        out_shape=jax.ShapeDtypeStruct((b, t, NAT_W), BF16),
        compiler_params=_params("arbitrary"),
        name="nat",
    )(pn, bias_tab)


def _rw_kernel(pr_ref, mu_ref, w0_ref, wd_ref, a0_ref, wa_ref, wg_ref, kk_ref, ka_ref, rk_ref,
               gnw_ref, gnb_ref, o_ref, s_ref, *, n_ctx, n_all):
    C = CHUNK
    C2 = 2 * C
    n_pairs = RW_W // LANES
    t_all = n_all * C
    s_ref[...] = jnp.zeros(s_ref.shape, F32)

    lane = _iota((1, LANES), 1)
    half = [lane < RW_DH, lane >= RW_DH]
    row_id = _iota((C, 1), 0)
    tri = [_tri_ones(0), _tri_ones(1)]
    bd = _block_ones(LANES, RW_DH)
    r_c = _iota((C, C2), 0)
    c_c = _mod_pow2(_iota((C, C2), 1), C)
    incl = [c_c <= r_c, c_c >= r_c]
    r_s = _iota((C2, C2), 0)
    c_s = _iota((C2, C2), 1)
    strict = [c_s < r_s, c_s > r_s]
    n_double = int(math.log2(C))
    top = _iota((C2, LANES), 0) < C
    st_half = top == (_iota((C2, LANES), 1) < RW_DH)

    def seg_sum(x):
        return _head_sums(x, bd)

    def lora(x, w_ref, direction):
        return _dot(_bf(x), w_ref[direction])

    def chunk(bi, c, direction, is_first, c_lo, n):
        r0 = pl.multiple_of(c * C, C)
        rs = pl.ds(r0, C)
        p = pr_ref[bi, rs, :].astype(F32)
        halo = 16
        prev_row = pr_ref[bi, pl.ds(pl.multiple_of(jnp.maximum(r0 - halo, 0), halo), halo), :].astype(F32)[
            halo - 1:halo, :]
        next_row = pr_ref[bi, pl.ds(pl.multiple_of(jnp.minimum(r0 + C, t_all - halo), halo), halo), :].astype(F32)[
            0:1, :]
        prev_row = jnp.where(c == c_lo, 0.0, prev_row)
        next_row = jnp.where(c == c_lo + n - 1, 0.0, next_row)
        up = jnp.where(row_id == 0, prev_row, pltpu.roll(p, 1, 0))
        dn = jnp.where(row_id == C - 1, next_row, pltpu.roll(p, C - 1, 0))
        y = p + (0.5 * (up + dn) - p) * mu_ref[...]

        r = y[:, 0:RW_W]
        k = y[:, RW_W:2 * RW_W]
        v = y[:, 2 * RW_W:3 * RW_W]
        dd = y[:, 3 * RW_W:3 * RW_W + LANES]
        ad = y[:, 3 * RW_W + LANES:3 * RW_W + 2 * LANES]

        other = 1 - direction
        d_lin = lora(jnp.tanh(dd), wd_ref, direction)
        a_lin = lora(ad, wa_ref, direction)
        kk = k * kk_ref[...]
        kk_ss = seg_sum(kk * kk)
        if not is_first:
            a_lin_o = lora(ad, wa_ref, other)
            gate = _dot(_bf(_sigmoid(y[:, 3 * RW_W + 2 * LANES:])), wg_ref[...])
        yield
        logw = -RW_DECAY_SCALE * _sigmoid(w0_ref[direction] + d_lin)
        asig = _sigmoid(a0_ref[direction] + a_lin)
        kk = kk * lax.rsqrt(kk_ss + 1e-12)
        kmod = k * (1.0 + (asig - 1.0) * ka_ref[...])
        gcum = _dot_exact_lhs(tri[direction], logw)
        if not is_first:
            kmod_o = k * (1.0 + (_sigmoid(a0_ref[other] + a_lin_o) - 1.0) * ka_ref[...])
            bonus_w = seg_sum(r * (kmod + kmod_o) * rk_ref[...])
        yield
        g_end = gcum[C - 1:C, :] if direction == 0 else gcum[0:1, :]
        e_pos = jnp.exp(gcum)
        e_neg = jnp.exp(-gcum)
        r_t = r * e_pos
        a_t = -kk * jnp.exp(gcum - logw)
        b_t = kk * asig * e_neg
        k_t = kmod * e_neg
        decay_end = jnp.exp(g_end)

        outs = [None] * n_pairs

        def pair(j):
            cl = slice(j * LANES, (j + 1) * LANES)

            def stack(x):
                xb = x[:, cl]
                return jnp.concatenate([jnp.where(half[0], xb, 0.0), jnp.where(half[1], xb, 0.0)], axis=0)

            a_st = stack(a_t)
            v_st = _bf(stack(v))
            rhs = _bf(jnp.concatenate([stack(b_t), stack(k_t)], axis=0))
            lhs = _bf(jnp.concatenate([r_t[:, cl], a_st], axis=0))
            sc = _dot(lhs, rhs, NT)
            yield
            a_rb = jnp.where(incl[direction], sc[0:C, 0:C2], 0.0)
            a_rk = jnp.where(incl[direction], sc[0:C, C2:], 0.0)
            n_bd = jnp.where(strict[direction], sc[C:, 0:C2], 0.0)
            a_ak = jnp.where(strict[direction], sc[C:, C2:], 0.0)
            y2 = _dot(_bf(a_ak), v_st)
            npow = _bf(n_bd)
            yield
            x = jnp.concatenate([a_st[0:C] + pltpu.roll(y2[0:C], RW_DH, 1),
                                 pltpu.roll(a_st[C:], RW_DH, 1) + y2[C:]], axis=0)
            for i in range(n_double):
                if i + 1 < n_double:
                    both = _dot(npow, jnp.concatenate([_bf(x), npow], axis=1))
                    yield
                    x = x + both[:, 0:LANES]
                    npow = _bf(both[:, LANES:])
                else:
                    upd = _dot(npow, _bf(x))
                    yield
                    x = x + upd
            x_sw = jnp.concatenate([pltpu.roll(x[0:C], RW_DH, 1), pltpu.roll(x[C:], RW_DH, 1)], axis=0)
            a_hat = jnp.where(st_half, jnp.where(top, x, x_sw), 0.0)
            u_hat = jnp.where(st_half, jnp.where(top, x_sw, x), 0.0)
            state = s_ref[bi, direction, j]
            m3 = _dot(_bf(jnp.concatenate([r_t[:, cl], a_hat], axis=0)), _bf(state), NT)
            yield
            u = m3[C:] + u_hat
            uv = jnp.concatenate([_bf(u), v_st], axis=0)
            o_j = _dot(_bf(jnp.concatenate([a_rb, a_rk], axis=1)), uv)
            s_new = _dot(uv, rhs, TN)
            yield
            outs[j] = m3[0:C] + o_j
            s_ref[bi, direction, j] = (state + s_new) * decay_end[:, cl]

        yield from _stages([pair(j) for j in range(n_pairs)])
        o = jnp.concatenate(outs, axis=1)

        if is_first:
            o_ref[bi, rs, :] = _bf(o)
        else:
            tot = o_ref[bi, rs, :].astype(F32) + o
            inv_n = 1.0 / RW_DH
            mean = seg_sum(tot) * inv_n
            yield
            xc = tot - mean
            var = seg_sum(xc * xc) * inv_n
            yield
            yn = xc * lax.rsqrt(var + RW_GN_EPS) * gnw_ref[...] + gnb_ref[...]
            o_ref[bi, rs, :] = _bf((yn + bonus_w * v) * gate)

    n_batch = pr_ref.shape[0]
    _run_two_way(chunk, 0, n_ctx, n_batch)
    _run_two_way(chunk, n_ctx, n_all - n_ctx, n_batch)


def _rwkv(pr, lc, mu, w0, wd, a0, wa, wg, k_k, k_a, r_k, gn_w, gn_b):
    b, t, _ = pr.shape
    nb = MIXER_BATCH
    kern = functools.partial(_rw_kernel, n_ctx=lc // CHUNK, n_all=t // CHUNK)
    consts = (mu, w0, wd, a0, wa, wg, k_k, k_a, r_k, gn_w, gn_b)
    return pl.pallas_call(
        kern,
        grid=(b // nb,),
        in_specs=[pl.BlockSpec((nb, t, RW_COLS_P), lambda i: (i, 0, 0))] + [_const_spec(a.shape) for a in consts],
        out_specs=pl.BlockSpec((nb, t, RW_W), lambda i: (i, 0, 0)),
        out_shape=jax.ShapeDtypeStruct((b, t, RW_W), BF16),
        scratch_shapes=[pltpu.VMEM((nb, 2, RW_W // LANES, LANES, LANES), F32)],
        compiler_params=_params("arbitrary"),
        name="rwkv7",
    )(pr, *consts)


def _post_kernel(*refs, n_x, nct, alpha, hidden, n_split):
    x_refs = refs[:n_x]
    a_ref, n_ref, c_ref, mod_ref, wo_ref, w13_ref, w2_ref, l1w_ref, l1b_ref, l2w_ref, l2b_ref, o_ref = refs[n_x:]
    nb, tm, _ = x_refs[0].shape
    stack = lambda ref: jnp.concatenate([_bf(ref[i]) for i in range(nb)], axis=0)
    y = (_dot(stack(a_ref), wo_ref[0:GLA_V, :])
         + _dot(stack(n_ref), wo_ref[GLA_V:GLA_V + NAT_W, :])
         + _dot(stack(c_ref), wo_ref[GLA_V + NAT_W:, :]))
    x1 = [_layer_norm(alpha * _token_rows(x_refs, i, nct) + mod_ref[i, 0, 2:3, :] * y[i * tm:(i + 1) * tm],
                      l1w_ref[...], l1b_ref[...]) for i in range(nb)]
    h = jnp.concatenate([_bf(x1[i] * (1.0 + mod_ref[i, 0, 4:5, :]) + mod_ref[i, 0, 3:4, :]) for i in range(nb)],
                        axis=0)
    hs = hidden // n_split
    f = None
    for k in range(n_split):
        gte = _dot(h, w13_ref[:, k * hs:(k + 1) * hs])
        up = _dot(h, w13_ref[:, hidden + k * hs:hidden + (k + 1) * hs])
        part = _dot(_bf(_silu(gte) * up), w2_ref[k * hs:(k + 1) * hs, :])
        f = part if f is None else f + part
    for i in range(nb):
        o_ref[i] = _layer_norm(alpha * x1[i] + mod_ref[i, 0, 5:6, :] * f[i * tm:(i + 1) * tm],
                               l2w_ref[...], l2b_ref[...])


def _post(xs, ya, yn, yc, modtab, w_out, w13, w2, ln1, ln2, tm, nct, row_off, alpha):
    b, _, d = xs[0].shape
    t = sum(a.shape[1] for a in xs)
    nb = DENSE_BATCH
    t_out = t - row_off * tm
    row = lambda bi, j: (bi, j + row_off, 0)
    seg = lambda bi, j: (bi, jnp.where(j + row_off >= nct, 1, 0), 0, 0)
    consts = (w_out, w13, w2, *ln1, *ln2)
    return pl.pallas_call(
        functools.partial(_post_kernel, n_x=len(xs), nct=nct, alpha=alpha, hidden=w2.shape[0], n_split=2),
        grid=(b // nb, t_out // tm),
        in_specs=_token_specs(xs, nb, tm, nct, row_off)
        + [pl.BlockSpec((nb, tm, GLA_V), row),
                  pl.BlockSpec((nb, tm, NAT_W), row),
                  pl.BlockSpec((nb, tm, RW_W), row),
                  pl.BlockSpec((nb, 1, 6, d), seg)] + [_const_spec(a.shape) for a in consts],
        out_specs=pl.BlockSpec((nb, tm, d), lambda bi, j: (bi, j, 0)),
        out_shape=jax.ShapeDtypeStruct((b, t_out, d), F32),
        compiler_params=_params("arbitrary", "arbitrary"),
        name="outproj_swiglu",
    )(*xs, ya, yn, yc, modtab, *consts)


def _rope_tables(lc, l):
    t = jnp.arange(l, dtype=jnp.int32)
    row = (t // GRID_W).astype(F32)
    col = (t % GRID_W).astype(F32)
    n_freq = GLA_DK // 4
    inv = ROPE_BASE ** (-jnp.arange(n_freq, dtype=F32) / n_freq)
    ang = jnp.concatenate([row[:, None] * inv, col[:, None] * inv], axis=-1)
    pair = (np.arange(LANES) % GLA_DK) // 2
    sign = np.where(np.arange(LANES) % 2 == 0, -1.0, 1.0).astype(np.float32)
    cos_l = jnp.cos(ang)[:, pair]
    sin_l = jnp.sin(ang)[:, pair] * sign
    cos_t = jnp.concatenate([jnp.ones((lc, LANES), F32), cos_l], axis=0)
    sin_t = jnp.concatenate([jnp.zeros((lc, LANES), F32), sin_l], axis=0)
    return cos_t, sin_t


def _nat_bias_table(rpb):
    KR, KC, W = NAT_WIN_R, NAT_WIN_C, GRID_W
    cq = np.arange(W)
    col0 = np.clip(cq - KC // 2, 0, W - KC)
    in_win = (cq[None, :] >= col0[:, None]) & (cq[None, :] < col0[:, None] + KC)
    dc = np.clip(cq[None, :] - cq[:, None], -(KC - 1), KC - 1) + KC - 1
    delta = np.arange(KR)
    dr = np.arange(KR)[None, :] - delta[:, None] + NAT_WIN_R - 1
    tab = rpb[:, dr][:, :, :, dc]
    tab = jnp.where(in_win[None, None, None], tab, -jnp.inf)
    tab = tab.transpose(0, 1, 3, 2, 4)
    return tab.reshape(NAT_HEADS, KR, W, KR * W)


def _pad_cols(w, width):
    return jnp.pad(w, ((0, 0), (0, width - w.shape[1])))


def _layer_params(i, w_in, gla_gate_up, gla_gate_b, gla_norm_w, nat_rpb, rw_mu, rw_w0, rw_wd2, rw_a0, rw_wa2,
                  rw_wg2, rw_k_k, rw_k_a, rw_r_k, rw_gn_w, rw_gn_b, w_out, ln1_w, ln1_b, ffn_w13, ffn_w2,
                  ln2_w, ln2_b):
    wi = w_in[i]
    o = 0
    pieces = []
    for size, width in ((GLA_QK, GLA_QK_PAD), (GLA_QK, GLA_QK_PAD), (GLA_V, GLA_V), (GLA_V, GLA_V),
                        (2 * GLA_GATE_RANK, LANES)):
        pieces.append(_pad_cols(wi[:, o:o + size], width))
        o += size
    pieces.append(wi[:, o:])
    p = {"w_in": _bf(jnp.concatenate(pieces, axis=1))}

    wg = jnp.zeros((LANES, 2 * GLA_QK_PAD), F32)
    wg = wg.at[0:GLA_GATE_RANK, 0:GLA_QK].set(gla_gate_up[i, 0])
    wg = wg.at[GLA_GATE_RANK:2 * GLA_GATE_RANK, GLA_QK_PAD:GLA_QK_PAD + GLA_QK].set(gla_gate_up[i, 1])
    p["gla_wg"] = _bf(wg)
    p["gla_gb"] = jnp.concatenate([_pad_cols(gla_gate_b[i, 0:1], GLA_QK_PAD),
                                   _pad_cols(gla_gate_b[i, 1:2], GLA_QK_PAD)], axis=1)
    p["gla_nw"] = jnp.tile(gla_norm_w[i], GLA_HEADS).reshape(1, GLA_V)
    p["nat_bias"] = _nat_bias_table(nat_rpb[i])

    def lora_pair(w, rank):
        z = jnp.zeros((2, LANES, RW_W), F32)
        z = z.at[0, 0:rank].set(w[0])
        return z.at[1, rank:2 * rank].set(w[1])

    row = lambda a: a.reshape(1, -1)
    p["rw"] = (row(rw_mu[i]), rw_w0[i].reshape(2, 1, RW_W), _bf(lora_pair(rw_wd2[i], RW_DECAY_RANK)),
               rw_a0[i].reshape(2, 1, RW_W), _bf(lora_pair(rw_wa2[i], RW_A_RANK)), _bf(rw_wg2[i]),
               row(rw_k_k[i]), row(rw_k_a[i]), row(rw_r_k[i]), row(rw_gn_w[i]), row(rw_gn_b[i]))
    p["w_out"] = _bf(w_out[i])
    p["ln1"] = (row(ln1_w[i]), row(ln1_b[i]))
    p["w13"] = _bf(ffn_w13[i])
    p["w2"] = _bf(ffn_w2[i])
    p["ln2"] = (row(ln2_w[i]), row(ln2_b[i]))
    return p


def kernel(x, c, ctx, c_ctx, w_mod, b_mod, w_in, gla_gate_up, gla_gate_b, gla_norm_w, nat_rpb, rw_mu, rw_w0,
           rw_wd2, rw_a0, rw_wa2, rw_wg2, rw_k_k, rw_k_a, rw_r_k, rw_gn_w, rw_gn_b, w_out, ln1_w, ln1_b,
           ffn_w13, ffn_w2, ln2_w, ln2_b):
    b, l, d = x.shape
    lc = ctx.shape[1]
    depth = w_mod.shape[0]
    alpha = (2.0 * depth) ** 0.25
    tm = 256 if lc % 256 == 0 else 128
    assert lc % tm == 0 and l % tm == 0 and lc % (2 * CHUNK) == 0 and l % (2 * CHUNK) == 0
    assert l % (NAT_ROWS_PER_STEP * GRID_W) == 0 and l // GRID_W >= NAT_WIN_R
    assert b % DENSE_BATCH == 0 and b % MIXER_BATCH == 0 and b % GLA_BATCH == 0
    nct = lc // tm

    cos_t, sin_t = _rope_tables(lc, l)
    rows = -(-(b + 1) // 8) * 8
    cvec = jnp.concatenate([c, c_ctx[None], jnp.zeros((rows - b - 1, d), F32)], axis=0)
    xs = (ctx, x) if depth > 1 else (jnp.concatenate([ctx, x], axis=1),)

    for i in range(depth):
        last = i == depth - 1
        p = _layer_params(i, w_in, gla_gate_up, gla_gate_b, gla_norm_w, nat_rpb, rw_mu, rw_w0, rw_wd2, rw_a0,
                          rw_wa2, rw_wg2, rw_k_k, rw_k_a, rw_r_k, rw_gn_w, rw_gn_b, w_out, ln1_w, ln1_b,
                          ffn_w13, ffn_w2, ln2_w, ln2_b)
        m = _modulation(cvec, w_mod[i], b_mod[i])
        ml = m[:b].reshape(b, 1, 6, d)
        mc = jnp.broadcast_to(m[b].reshape(1, 1, 6, d), (b, 1, 6, d))
        modtab = jnp.concatenate([mc, ml], axis=1)

        pg, pn, pr = _inproj(xs, modtab, p["w_in"], cos_t, sin_t, tm, nct)
        ya = _gla(pg, p["gla_wg"], p["gla_gb"], p["gla_nw"], lc)
        yn = _nat(pn, p["nat_bias"], lc)
        yc = _rwkv(pr, lc, *p["rw"])
        row_off = nct if last else 0
        xs = (_post(xs, ya, yn, yc, modtab, p["w_out"], p["w13"], p["w2"], p["ln1"], p["ln2"], tm, nct, row_off,
                    alpha),)
    return xs[0]
```
